```python
import math
import jax, jax.numpy as jnp
from jax import lax
import numpy as np

D_MODEL = 1024
BATCH = 32
SEQ = 2048
DEPTH = 1

DN_HEADS = D_MODEL // 256
DN_DK = 128
DN_DV = 128
CONV_WIDTH = 4
CHUNK = 64
DIFF_HEADS = D_MODEL // 256
DIFF_DQK = 64
DIFF_DV = 2 * DIFF_DQK
ROPE_THETA = 500000.0
ROPE_DIM = DIFF_DQK // 4
Q_BLOCK = 128
D_FF = 4 * D_MODEL
EPS = 1e-6

DN_QK = DN_HEADS * DN_DK
DN_V = DN_HEADS * DN_DV
DN_CONV = 2 * DN_QK + DN_V
DF_QK = DIFF_HEADS * 2 * DIFF_DQK
DF_V = DIFF_HEADS * DIFF_DV
MIX_WIDTH = DN_V + DF_V
D_IN = DN_CONV + DN_V + 2 * DN_HEADS + 2 * DF_QK + DF_V

kernel_name = "hymba_gated_deltanet_diff_attn_layer"


def rmsnorm(x, w):
    x32 = x.astype(jnp.float32)
    y = x32 * lax.rsqrt(jnp.mean(x32 * x32, axis=-1, keepdims=True) + EPS)
    return (y * w.astype(jnp.float32)).astype(x.dtype)


def l2norm(x):
    return x * lax.rsqrt(jnp.sum(x * x, axis=-1, keepdims=True) + 1e-6)


def rotary_tables(positions):
    inv_freq = ROPE_THETA ** (-jnp.arange(0, ROPE_DIM, 2, dtype=jnp.float32) / ROPE_DIM)
    ang = positions.astype(jnp.float32)[:, None] * inv_freq[None, :]
    return jnp.cos(ang), jnp.sin(ang)


def partial_rotary(x, cos, sin):
    half = ROPE_DIM // 2
    c = cos.astype(x.dtype)
    s = sin.astype(x.dtype)
    x1 = x[..., :half]
    x2 = x[..., half:ROPE_DIM]
    return jnp.concatenate([x1 * c - x2 * s, x2 * c + x1 * s, x[..., ROPE_DIM:]], axis=-1)


def causal_depthwise_conv(x, w):
    rhs = w[:, None, :].astype(x.dtype)
    return lax.conv_general_dilated(
        x, rhs, window_strides=(1,), padding=[(CONV_WIDTH - 1, 0)],
        dimension_numbers=("NWC", "WIO", "NWC"), feature_group_count=x.shape[-1])


def gated_delta_rule_chunked(q, k, v, g, beta):
    b, seq, h, dk = q.shape
    dv = v.shape[-1]
    n = seq // CHUNK
    q = l2norm(q) * (dk ** -0.5)
    k = l2norm(k)

    def chunks(t):
        return t.reshape(b, n, CHUNK, h, -1).transpose(0, 3, 1, 2, 4)

    q, k, v = chunks(q), chunks(k), chunks(v)
    g = jnp.cumsum(g.reshape(b, n, CHUNK, h).transpose(0, 3, 1, 2), axis=-1)
    beta = beta.reshape(b, n, CHUNK, h).transpose(0, 3, 1, 2)

    idx = jnp.arange(CHUNK)
    incl = idx[:, None] >= idx[None, :]
    strict = idx[:, None] > idx[None, :]
    decay = jnp.exp(jnp.where(incl, g[..., :, None] - g[..., None, :], -jnp.inf))

    kb = k * beta[..., None]
    a_strict = jnp.einsum("bhnid,bhnjd->bhnij", kb, k) * jnp.where(strict, decay, 0.0)
    lhs = a_strict + jnp.eye(CHUNK, dtype=a_strict.dtype)
    u = lax.linalg.triangular_solve(lhs, v * beta[..., None], left_side=True, lower=True,
                                    unit_diagonal=True)
    w = lax.linalg.triangular_solve(lhs, kb * jnp.exp(g)[..., None], left_side=True,
                                    lower=True, unit_diagonal=True)
    qk = jnp.einsum("bhnid,bhnjd->bhnij", q, k) * decay
    g_last = g[..., -1:]
    q_state = q * jnp.exp(g)[..., None]
    k_state = k * jnp.exp(g_last - g)[..., None]
    chunk_decay = jnp.exp(g_last[..., 0])

    xs = tuple(jnp.moveaxis(t, 2, 0) for t in (q_state, k_state, u, w, qk, chunk_decay))

    def step(state, inp):
        qs, ks, u_n, w_n, qk_n, dec = inp
        v_new = u_n - jnp.einsum("bhcd,bhde->bhce", w_n, state)
        o = jnp.einsum("bhcd,bhde->bhce", qs, state) + jnp.einsum("bhij,bhje->bhie", qk_n, v_new)
        state = state * dec[..., None, None] + jnp.einsum("bhcd,bhce->bhde", ks, v_new)
        return state, o

    s0 = jnp.zeros((b, h, dk, dv), jnp.float32)
    _, o = lax.scan(step, s0, xs)
    return o.transpose(1, 0, 3, 2, 4).reshape(b, seq, h, dv)


def diff_attention(q1, q2, k1, k2, v, lam):
    b, h, seq, _ = q1.shape
    nb = seq // Q_BLOCK
    scale = DIFF_DQK ** -0.5
    kpos = jnp.arange(seq)

    def block(i):
        start = i * Q_BLOCK
        qpos = start + jnp.arange(Q_BLOCK)
        mask = kpos[None, :] <= qpos[:, None]

        def probs(q, k):
            qb = lax.dynamic_slice_in_dim(q, start, Q_BLOCK, axis=2)
            s = jnp.einsum("bhqd,bhkd->bhqk", qb, k).astype(jnp.float32) * scale
            return jax.nn.softmax(jnp.where(mask, s, -jnp.inf), axis=-1)

        att = probs(q1, k1) - lam * probs(q2, k2)
        return jnp.einsum("bhqk,bhkd->bhqd", att.astype(v.dtype), v)

    out = lax.map(block, jnp.arange(nb))
    return out.transpose(1, 2, 0, 3, 4).reshape(b, h, seq, v.shape[-1])


def hybrid_layer(x, cos, sin, layer_idx, attn_norm_w, w_in, conv_w, a_log, dt_bias,
                 dn_norm_w, lambda_q1, lambda_k1, lambda_q2, lambda_k2, diff_norm_w,
                 group_scale, w_out, mlp_norm_w, w_up, w_down):
    f32 = jnp.float32
    b, seq, _ = x.shape
    hn = rmsnorm(x, attn_norm_w)
    proj = hn @ w_in
    cuts = np.cumsum([DN_CONV, DN_V, DN_HEADS, DN_HEADS, DF_QK, DF_QK]).tolist()
    dn_qkv, dn_z, dn_b, dn_a, df_q, df_k, df_v = jnp.split(proj, cuts, axis=-1)

    dn_qkv = jax.nn.silu(causal_depthwise_conv(dn_qkv, conv_w))
    dq, dk, dv = jnp.split(dn_qkv, [DN_QK, 2 * DN_QK], axis=-1)
    dq = dq.reshape(b, seq, DN_HEADS, DN_DK).astype(f32)
    dk = dk.reshape(b, seq, DN_HEADS, DN_DK).astype(f32)
    dv = dv.reshape(b, seq, DN_HEADS, DN_DV).astype(f32)
    beta = jax.nn.sigmoid(dn_b.astype(f32))
    g = -jnp.exp(a_log.astype(f32)) * jax.nn.softplus(dn_a.astype(f32) + dt_bias.astype(f32))
    o_dn = gated_delta_rule_chunked(dq, dk, dv, g, beta)
    z = dn_z.reshape(b, seq, DN_HEADS, DN_DV).astype(f32)
    o_dn = (rmsnorm(o_dn, dn_norm_w) * jax.nn.silu(z)).reshape(b, seq, DN_V).astype(x.dtype)

    q = df_q.reshape(b, seq, DIFF_HEADS, 2, DIFF_DQK).transpose(0, 2, 3, 1, 4)
    k = df_k.reshape(b, seq, DIFF_HEADS, 2, DIFF_DQK).transpose(0, 2, 3, 1, 4)
    q = partial_rotary(q, cos, sin)
    k = partial_rotary(k, cos, sin)
    v = df_v.reshape(b, seq, DIFF_HEADS, DIFF_DV).transpose(0, 2, 1, 3)
    lambda_init = 0.8 - 0.6 * math.exp(-0.3 * layer_idx)
    lam = (jnp.exp(jnp.sum(lambda_q1.astype(f32) * lambda_k1.astype(f32)))
           - jnp.exp(jnp.sum(lambda_q2.astype(f32) * lambda_k2.astype(f32))) + lambda_init)
    o_df = diff_attention(q[:, :, 0], q[:, :, 1], k[:, :, 0], k[:, :, 1], v, lam)
    o_df = rmsnorm(o_df, diff_norm_w) * (1.0 - lambda_init)
    o_df = o_df.transpose(0, 2, 1, 3).reshape(b, seq, DF_V)

    mixed = jnp.concatenate([o_dn, o_df], axis=-1) * group_scale
    x = x + mixed @ w_out

    hm = rmsnorm(x, mlp_norm_w)
    x = x + jnp.square(jax.nn.relu(hm @ w_up)) @ w_down
    return x


def setup_inputs(seed: int = 0) -> dict:
    key = jax.random.key(seed)
    ks = jax.random.split(key, 20)
    f32 = jnp.float32

    def nrm(k, shape, scale):
        return jax.random.normal(k, shape, f32) * scale

    def gain(k, shape):
        return 1.0 + 0.02 * jax.random.normal(k, shape, f32)

    dt = jnp.exp(jax.random.uniform(ks[5], (DEPTH, DN_HEADS), f32, math.log(1e-3), math.log(1e-1)))
    return {
        "x": jax.random.normal(ks[0], (BATCH, SEQ, D_MODEL), f32),
        "positions": jnp.arange(SEQ, dtype=jnp.int32),
        "attn_norm_w": gain(ks[1], (DEPTH, D_MODEL)),
        "w_in": nrm(ks[2], (DEPTH, D_MODEL, D_IN), D_MODEL ** -0.5),
        "conv_w": nrm(ks[3], (DEPTH, CONV_WIDTH, DN_CONV), CONV_WIDTH ** -0.5),
        "a_log": jnp.log(jax.random.uniform(ks[4], (DEPTH, DN_HEADS), f32, 1.0, 16.0)),
        "dt_bias": dt + jnp.log(-jnp.expm1(-dt)),
        "dn_norm_w": gain(ks[6], (DEPTH, DN_DV)),
        "lambda_q1": nrm(ks[7], (DEPTH, DIFF_DQK), 0.1),
        "lambda_k1": nrm(ks[8], (DEPTH, DIFF_DQK), 0.1),
        "lambda_q2": nrm(ks[9], (DEPTH, DIFF_DQK), 0.1),
        "lambda_k2": nrm(ks[10], (DEPTH, DIFF_DQK), 0.1),
        "diff_norm_w": gain(ks[11], (DEPTH, DIFF_DV)),
        "group_scale": gain(ks[12], (DEPTH, MIX_WIDTH)),
        "w_out": nrm(ks[13], (DEPTH, MIX_WIDTH, D_MODEL), MIX_WIDTH ** -0.5),
        "mlp_norm_w": gain(ks[14], (DEPTH, D_MODEL)),
        "w_up": nrm(ks[15], (DEPTH, D_MODEL, D_FF), D_MODEL ** -0.5),
        "w_down": nrm(ks[16], (DEPTH, D_FF, D_MODEL), D_FF ** -0.5),
        "final_norm_w": gain(ks[17], (D_MODEL,)),
    }


def reference(x, positions, attn_norm_w, w_in, conv_w, a_log, dt_bias, dn_norm_w,
              lambda_q1, lambda_k1, lambda_q2, lambda_k2, diff_norm_w, group_scale,
              w_out, mlp_norm_w, w_up, w_down, final_norm_w):
    cos, sin = rotary_tables(positions)
    for l in range(DEPTH):
        x = hybrid_layer(x, cos, sin, l, attn_norm_w[l], w_in[l], conv_w[l], a_log[l],
                         dt_bias[l], dn_norm_w[l], lambda_q1[l], lambda_k1[l],
                         lambda_q2[l], lambda_k2[l], diff_norm_w[l], group_scale[l],
                         w_out[l], mlp_norm_w[l], w_up[l], w_down[l])
    return rmsnorm(x, final_norm_w)
```

```python
import functools
import math

import jax
import jax.numpy as jnp
from jax import lax
from jax.experimental import pallas as pl
from jax.experimental.pallas import tpu as pltpu

F32 = jnp.float32
BF16 = jnp.bfloat16

D_MODEL = 1024
DN_HEADS = 4
DN_DK = 128
DN_DV = 128
CONV_WIDTH = 4
CHUNK = 64
DIFF_HEADS = 4
DIFF_DQK = 64
DIFF_DV = 128
ROPE_THETA = 500000.0
ROPE_DIM = 16
D_FF = 4 * D_MODEL
EPS = 1e-6

DN_QK = DN_HEADS * DN_DK
DN_V = DN_HEADS * DN_DV
DN_CONV = 2 * DN_QK + DN_V
DF_QK = DIFF_HEADS * 2 * DIFF_DQK
DF_V = DIFF_HEADS * DIFF_DV
GATE_OFF = DN_CONV + DN_V
N_GATES = 2 * DN_HEADS
MAIN_N = DN_CONV + DN_V + 2 * DF_QK + DF_V
LANES = 128
CONV_PAD = 8

VMEM_LIMIT = 56 * 1024 * 1024
TOKEN_TILE = 512
ATTN_BLOCK = 512
FF_CHUNK = 1024

HI = lax.Precision.HIGHEST


def _dot(a, b, precision=None):
    return jnp.dot(a, b, preferred_element_type=F32, precision=precision)


def _dot_nt(a, b, precision=None):
    return lax.dot_general(a, b, (((1,), (1,)), ((), ())), preferred_element_type=F32,
                           precision=precision)


def _dot_tn(a, b, precision=None):
    return lax.dot_general(a, b, (((0,), (0,)), ((), ())), preferred_element_type=F32,
                           precision=precision)


def _rms(x, w):
    return x * lax.rsqrt(jnp.mean(x * x, axis=-1, keepdims=True) + EPS) * w


def _silu(x):
    return x / (1.0 + jnp.exp(-x))


def _resident(shape):
    zeros = (0,) * len(shape)
    return pl.BlockSpec(shape, lambda *_: zeros, pipeline_mode=pl.Buffered(1))


def _rope_table_kernel(pos_ref, invf_ref, c_ref, s1_ref, s2_ref):
    ang = pos_ref[...] * invf_ref[...]
    lane = lax.broadcasted_iota(jnp.int32, ang.shape, 1) & (DIFF_DQK - 1)
    c = jnp.cos(ang)
    s = jnp.sin(ang)
    half = ROPE_DIM // 2
    c_ref[...] = jnp.where(lane < ROPE_DIM, c, 1.0)
    s1_ref[...] = jnp.where((lane >= half) & (lane < ROPE_DIM), s, 0.0)
    s2_ref[...] = jnp.where(lane < half, -s, 0.0)


def _rope_tables(positions):
    seq = positions.shape[0]
    half = ROPE_DIM // 2
    inv_freq = ROPE_THETA ** (-jnp.arange(0, ROPE_DIM, 2, dtype=F32) / ROPE_DIM)
    lane = jnp.arange(LANES) % DIFF_DQK
    idx = jnp.where(lane < half, lane, jnp.where(lane < ROPE_DIM, lane - half, 0))
    invf = inv_freq[idx][None, :]
    pos = jnp.broadcast_to(positions.astype(F32)[:, None], (seq, LANES))
    table = jax.ShapeDtypeStruct((seq, LANES), F32)
    return pl.pallas_call(
        _rope_table_kernel,
        out_shape=(table, table, table),
        name="rope_tables",
    )(pos, invf)


def _inproj_kernel(x_ref, nw_ref, w_ref, wg_ref, c_ref, s1_ref, s2_ref, alog_ref, dtb_ref,
                   qkv_ref, z_ref, gate_ref, qk_ref, v_ref):
    hn = _rms(x_ref[...], nw_ref[...]).astype(BF16)
    qkv_ref[...] = _dot(hn, w_ref[:, 0:DN_CONV]).astype(BF16)
    z_ref[...] = _dot(hn, w_ref[:, DN_CONV:DN_CONV + DN_V]).astype(BF16)
    off = DN_CONV + DN_V
    qk = _dot(hn, w_ref[:, off:off + 2 * DF_QK])
    c, s1, s2 = c_ref[...], s1_ref[...], s2_ref[...]
    half = ROPE_DIM // 2
    for i in range(2 * DF_QK // LANES):
        blk = qk[:, i * LANES:(i + 1) * LANES]
        rot = (blk * c + pltpu.roll(blk, half, 1) * s1 + pltpu.roll(blk, LANES - half, 1) * s2)
        qk_ref[:, i * LANES:(i + 1) * LANES] = rot.astype(BF16)
    off += 2 * DF_QK
    v_ref[...] = _dot(hn, w_ref[:, off:off + DF_V]).astype(BF16)
    gpre = _dot(hn, wg_ref[...])
    lane = lax.broadcasted_iota(jnp.int32, gpre.shape, 1)
    beta = 1.0 / (1.0 + jnp.exp(-gpre))
    t = gpre + dtb_ref[...]
    softplus = jnp.maximum(t, 0.0) + jnp.log1p(jnp.exp(-jnp.abs(t)))
    g = -jnp.exp(alog_ref[...]) * softplus
    gate_ref[...] = jnp.where(lane < DN_HEADS, beta, g)


def _input_projection(x2d, seq, norm_w, w_main, w_gate, tables, alog_row, dtb_row):
    tokens = x2d.shape[0]
    tm = min(TOKEN_TILE, seq)
    tiles_per_seq = seq // tm
    row = lambda i: (i, 0)
    pos = lambda i: (i % tiles_per_seq, 0)
    out_shape = (
        jax.ShapeDtypeStruct((tokens, DN_CONV), BF16),
        jax.ShapeDtypeStruct((tokens, DN_V), BF16),
        jax.ShapeDtypeStruct((tokens, LANES), F32),
        jax.ShapeDtypeStruct((tokens, 2 * DF_QK), BF16),
        jax.ShapeDtypeStruct((tokens, DF_V), BF16),
    )
    return pl.pallas_call(
        _inproj_kernel,
        grid=(tokens // tm,),
        in_specs=[
            pl.BlockSpec((tm, D_MODEL), row),
            _resident((1, D_MODEL)),
            _resident((D_MODEL, MAIN_N)),
            _resident((D_MODEL, LANES)),
            pl.BlockSpec((tm, LANES), pos),
            pl.BlockSpec((tm, LANES), pos),
            pl.BlockSpec((tm, LANES), pos),
            _resident((1, LANES)),
            _resident((1, LANES)),
        ],
        out_specs=(
            pl.BlockSpec((tm, DN_CONV), row),
            pl.BlockSpec((tm, DN_V), row),
            pl.BlockSpec((tm, LANES), row),
            pl.BlockSpec((tm, 2 * DF_QK), row),
            pl.BlockSpec((tm, DF_V), row),
        ),
        out_shape=out_shape,
        compiler_params=pltpu.CompilerParams(
            dimension_semantics=("arbitrary",), vmem_limit_bytes=VMEM_LIMIT),
        name="input_projection",
    )(x2d, norm_w, w_main, w_gate, *tables, alog_row, dtb_row)


def _deltanet_kernel(qkv_ref, z_ref, gate_ref, convw_ref, nw_ref, gs_ref, out_ref,
                     xpad_ref, w_s, u_s, qk_s, qs_s, ks_s, dec_s, state_ref, *, seq):
    n_chunks = seq // CHUNK
    C = CHUNK
    ii = lax.broadcasted_iota(jnp.int32, (C, C), 0)
    jj = lax.broadcasted_iota(jnp.int32, (C, C), 1)
    tril = (ii >= jj).astype(F32)
    eye = (ii == jj).astype(F32)
    lane = lax.broadcasted_iota(jnp.int32, (C, LANES), 1)

    def conv_silu(win, cw):
        y = cw[CONV_WIDTH - 1:CONV_WIDTH, :] * win[CONV_PAD:, :]
        for shift in range(1, CONV_WIDTH):
            tap = CONV_WIDTH - 1 - shift
            y = y + cw[tap:tap + 1, :] * pltpu.roll(win, shift, 0)[CONV_PAD:, :]
        return _silu(y)

    for h in range(DN_HEADS):
        xpad_ref[0:CONV_PAD, :] = jnp.zeros((CONV_PAD, 3 * LANES), F32)
        for part in range(3):
            col = part * DN_QK + h * LANES
            xpad_ref[CONV_PAD:, part * LANES:(part + 1) * LANES] = (
                qkv_ref[:, col:col + LANES].astype(F32))
        cw_q = convw_ref[:, h * LANES:(h + 1) * LANES]
        cw_k = convw_ref[:, DN_QK + h * LANES:DN_QK + (h + 1) * LANES]
        cw_v = convw_ref[:, 2 * DN_QK + h * LANES:2 * DN_QK + (h + 1) * LANES]

        def precompute(c, carry, h=h, cw_q=cw_q, cw_k=cw_k, cw_v=cw_v):
            r0 = pl.multiple_of(c * C, C)
            win = xpad_ref[pl.ds(r0, C + CONV_PAD), :]
            q = conv_silu(win[:, 0:LANES], cw_q)
            k = conv_silu(win[:, LANES:2 * LANES], cw_k)
            v = conv_silu(win[:, 2 * LANES:3 * LANES], cw_v)
            qn = q * lax.rsqrt(jnp.sum(q * q, axis=-1, keepdims=True) + 1e-6) * (DN_DK ** -0.5)
            kn = k * lax.rsqrt(jnp.sum(k * k, axis=-1, keepdims=True) + 1e-6)

            gates = gate_ref[pl.ds(r0, C), :]
            beta = jnp.sum(jnp.where(lane == h, gates, 0.0), axis=-1, keepdims=True)
            g = jnp.sum(jnp.where(lane == DN_HEADS + h, gates, 0.0), axis=-1, keepdims=True)
            gcum = _dot(tril, jnp.broadcast_to(g, (C, LANES)), HI)
            g_col = gcum[:, 0:C]
            g_row = gcum.T[0:C, :]
            delta = g_col - g_row
            dec_incl = jnp.exp(jnp.where(ii >= jj, delta, -1e30))
            dec_strict = jnp.where(ii > jj, dec_incl, 0.0)
            g_last = gcum[C - 1:C, :]
            eg = jnp.exp(gcum)
            ekl = jnp.exp(g_last - gcum)

            kb = kn * beta
            knb = kn.astype(BF16)
            a = _dot_nt(kb.astype(BF16), knb) * dec_strict
            n_pow = -a
            t_inv = eye + n_pow
            for _ in range(int(math.log2(C)) - 1):
                n_pow = _dot(n_pow, n_pow, HI)
                t_inv = t_inv + _dot(t_inv, n_pow, HI)
            rhs = jnp.concatenate([kb * eg, v * beta], axis=1)
            wu = _dot(t_inv, rhs, HI)
            qk = _dot_nt(qn.astype(BF16), knb) * dec_incl

            w_s[h, pl.ds(r0, C), :] = wu[:, 0:LANES].astype(BF16)
            u_s[h, pl.ds(r0, C), :] = wu[:, LANES:2 * LANES]
            qk_s[h, pl.ds(r0, C), :] = qk.astype(BF16)
            qs_s[h, pl.ds(r0, C), :] = (qn * eg).astype(BF16)
            ks_s[h, pl.ds(r0, C), :] = (kn * ekl).astype(BF16)
            dec_s[h, c] = jnp.broadcast_to(jnp.exp(g_last), (8, LANES))
            return carry

        lax.fori_loop(0, n_chunks, precompute, 0)

    state_ref[...] = jnp.zeros(state_ref.shape, F32)
    nw = nw_ref[...]

    def recur(c, carry):
        r0 = pl.multiple_of(c * C, C)
        rows = pl.ds(r0, C)
        for h in range(DN_HEADS):
            s = state_ref[h]
            sb = s.astype(BF16)
            v_new = u_s[h, rows, :] - _dot(w_s[h, rows, :], sb)
            vb = v_new.astype(BF16)
            o = _dot(qs_s[h, rows, :], sb) + _dot(qk_s[h, rows, :], vb)
            state_ref[h] = s * dec_s[h, c][0:1, :] + _dot_tn(ks_s[h, rows, :], vb)
            z = z_ref[rows, h * LANES:(h + 1) * LANES].astype(F32)
            y = _rms(o, nw) * _silu(z) * gs_ref[:, h * LANES:(h + 1) * LANES]
            out_ref[rows, h * LANES:(h + 1) * LANES] = y.astype(BF16)
        return carry

    lax.fori_loop(0, n_chunks, recur, 0)


def _deltanet(qkv, z, gates, conv_w, norm_w, gs):
    batch, seq, _ = qkv.shape
    n_chunks = seq // CHUNK
    per_batch = lambda width: pl.BlockSpec((None, seq, width), lambda b: (b, 0, 0))
    return pl.pallas_call(
        functools.partial(_deltanet_kernel, seq=seq),
        grid=(batch,),
        in_specs=[
            per_batch(DN_CONV),
            per_batch(DN_V),
            per_batch(LANES),
            _resident((CONV_WIDTH, DN_CONV)),
            _resident((1, DN_DV)),
            _resident((1, DN_V)),
        ],
        out_specs=per_batch(DN_V),
        out_shape=jax.ShapeDtypeStruct((batch, seq, DN_V), BF16),
        scratch_shapes=[
            pltpu.VMEM((seq + CONV_PAD, 3 * LANES), F32),
            pltpu.VMEM((DN_HEADS, seq, DN_DK), BF16),
            pltpu.VMEM((DN_HEADS, seq, DN_DV), F32),
            pltpu.VMEM((DN_HEADS, seq, CHUNK), BF16),
            pltpu.VMEM((DN_HEADS, seq, DN_DK), BF16),
            pltpu.VMEM((DN_HEADS, seq, DN_DK), BF16),
            pltpu.VMEM((DN_HEADS, n_chunks, 8, LANES), F32),
            pltpu.VMEM((DN_HEADS, DN_DK, DN_DV), F32),
        ],
        compiler_params=pltpu.CompilerParams(
            dimension_semantics=("arbitrary",), vmem_limit_bytes=VMEM_LIMIT),
        name="gated_deltanet",
    )(qkv, z, gates, conv_w, norm_w, gs)


def _diff_attn_kernel(lam_ref, nw_ref, gs_ref, q_ref, k_ref, v_ref, o_ref, *, seq, block,
                      lambda_init):
    lp = lam_ref[...]
    lam = (jnp.exp(jnp.sum(lp[0:1, :] * lp[1:2, :], axis=-1, keepdims=True))
           - jnp.exp(jnp.sum(lp[2:3, :] * lp[3:4, :], axis=-1, keepdims=True)) + lambda_init)
    lane = lax.broadcasted_iota(jnp.int32, (block, LANES), 1)
    first_map = lane < DIFF_DQK
    row = lax.broadcasted_iota(jnp.int32, (block, block), 0)
    col = lax.broadcasted_iota(jnp.int32, (block, block), 1)
    causal = row >= col
    scale = DIFF_DQK ** -0.5
    zero = jnp.zeros((), BF16)

    def update(carry, q1, q2, kb, vb, mask):
        new = []
        for qm, (m, l, acc) in zip((q1, q2), carry):
            s = _dot_nt(qm, kb)
            if mask is not None:
                s = jnp.where(mask, s, -1e30)
            m_new = jnp.maximum(m, jnp.max(s, axis=-1, keepdims=True))
            alpha = jnp.exp(m - m_new)
            p = jnp.exp(s - m_new)
            l_new = alpha * l + jnp.sum(p, axis=-1, keepdims=True)
            acc_new = alpha * acc + _dot(p.astype(BF16), vb)
            new.append((m_new, l_new, acc_new))
        return tuple(new)

    for qi in range(seq // block):
        q = q_ref[qi * block:(qi + 1) * block, :] * jnp.asarray(scale, BF16)
        q1 = jnp.where(first_map, q, zero)
        q2 = jnp.where(first_map, zero, q)
        init_map = (jnp.full((block, 1), -1e30, F32), jnp.zeros((block, 1), F32),
                    jnp.zeros((block, DIFF_DV), F32))

        def body(j, carry, q1=q1, q2=q2):
            r0 = pl.multiple_of(j * block, block)
            return update(carry, q1, q2, k_ref[pl.ds(r0, block), :], v_ref[pl.ds(r0, block), :],
                          None)

        carry = lax.fori_loop(0, qi, body, (init_map, init_map))
        rows = slice(qi * block, (qi + 1) * block)
        (_, l1, a1), (_, l2, a2) = update(carry, q1, q2, k_ref[rows, :], v_ref[rows, :], causal)
        o = a1 / l1 - lam * (a2 / l2)
        y = _rms(o, nw_ref[...]) * (1.0 - lambda_init) * gs_ref[...]
        o_ref[rows, :] = y.astype(BF16)


def _diff_attention(qk, v, lam_params, norm_w, gs, lambda_init):
    batch, seq, _ = v.shape
    block = min(ATTN_BLOCK, seq)
    head_cols = lambda off: pl.BlockSpec((None, seq, LANES), lambda b, h: (b, 0, off + h))
    return pl.pallas_call(
        functools.partial(_diff_attn_kernel, seq=seq, block=block, lambda_init=lambda_init),
        grid=(batch, DIFF_HEADS),
        in_specs=[
            _resident((4, DIFF_DQK)),
            _resident((1, DIFF_DV)),
            pl.BlockSpec((1, LANES), lambda b, h: (0, h)),
            head_cols(0),
            head_cols(DIFF_HEADS),
            head_cols(0),
        ],
        out_specs=head_cols(0),
        out_shape=jax.ShapeDtypeStruct((batch, seq, DF_V), BF16),
        compiler_params=pltpu.CompilerParams(
            dimension_semantics=("arbitrary", "arbitrary"), vmem_limit_bytes=VMEM_LIMIT),
        name="diff_attention",
    )(lam_params, norm_w, gs, qk, qk, v)


def _out_mlp_kernel(x_ref, odn_ref, odf_ref, wo_ref, nw_ref, wu_ref, wd_ref, fw_ref, y_ref,
                    hidden_ref, *, final_norm):
    x1 = (x_ref[...] + _dot(odn_ref[...], wo_ref[0:DN_V, :])
          + _dot(odf_ref[...], wo_ref[DN_V:DN_V + DF_V, :]))
    hm = _rms(x1, nw_ref[...]).astype(BF16)
    for c in range(D_FF // FF_CHUNK):
        cols = slice(c * FF_CHUNK, (c + 1) * FF_CHUNK)
        hidden_ref[:, cols] = jnp.square(jnp.maximum(_dot(hm, wu_ref[:, cols]), 0.0)).astype(BF16)
    y = x1 + _dot(hidden_ref[...], wd_ref[...])
    if final_norm:
        y = _rms(y, fw_ref[...])
    y_ref[...] = y


def _out_mlp(x2d, o_dn, o_df, w_out, norm_w, w_up, w_down, final_w, final_norm):
    tokens = x2d.shape[0]
    tm = min(TOKEN_TILE, tokens)
    row = lambda i: (i, 0)
    return pl.pallas_call(
        functools.partial(_out_mlp_kernel, final_norm=final_norm),
        grid=(tokens // tm,),
        in_specs=[
            pl.BlockSpec((tm, D_MODEL), row),
            pl.BlockSpec((tm, DN_V), row),
            pl.BlockSpec((tm, DF_V), row),
            _resident((DN_V + DF_V, D_MODEL)),
            _resident((1, D_MODEL)),
            _resident((D_MODEL, D_FF)),
            _resident((D_FF, D_MODEL)),
            _resident((1, D_MODEL)),
        ],
        out_specs=pl.BlockSpec((tm, D_MODEL), row),
        out_shape=jax.ShapeDtypeStruct((tokens, D_MODEL), F32),
        scratch_shapes=[pltpu.VMEM((tm, D_FF), BF16)],
        compiler_params=pltpu.CompilerParams(
            dimension_semantics=("arbitrary",), vmem_limit_bytes=VMEM_LIMIT),
        name="out_projection_mlp",
    )(x2d, o_dn, o_df, w_out, norm_w, w_up, w_down, final_w)


def _pad_lanes(vec, offset):
    return jnp.zeros((1, LANES), F32).at[0, offset:offset + vec.shape[0]].set(vec.astype(F32))


def kernel(x, positions, attn_norm_w, w_in, conv_w, a_log, dt_bias, dn_norm_w, lambda_q1, lambda_k1, lambda_q2, lambda_k2, diff_norm_w, group_scale, w_out, mlp_norm_w, w_up, w_down, final_norm_w):
    batch, seq, _ = x.shape
    depth = w_in.shape[0]
    tables = _rope_tables(positions)
    x2d = x.reshape(batch * seq, D_MODEL)
    for l in range(depth):
        wl = w_in[l]
        w_main = jnp.concatenate([wl[:, :GATE_OFF], wl[:, GATE_OFF + N_GATES:]], axis=1).astype(BF16)
        w_gate = jnp.pad(wl[:, GATE_OFF:GATE_OFF + N_GATES], ((0, 0), (0, LANES - N_GATES))).astype(BF16)
        qkv, z, gates, qk, v = _input_projection(
            x2d, seq, attn_norm_w[l][None, :], w_main, w_gate, tables,
            _pad_lanes(a_log[l], DN_HEADS), _pad_lanes(dt_bias[l], DN_HEADS))
        gs = group_scale[l].astype(F32)[None, :]
        o_dn = _deltanet(qkv.reshape(batch, seq, DN_CONV), z.reshape(batch, seq, DN_V),
                         gates.reshape(batch, seq, LANES), conv_w[l].astype(F32),
                         dn_norm_w[l].astype(F32)[None, :], gs[:, :DN_V])
        lam_params = jnp.stack([lambda_q1[l], lambda_k1[l], lambda_q2[l], lambda_k2[l]]).astype(F32)
        lambda_init = 0.8 - 0.6 * math.exp(-0.3 * l)
        o_df = _diff_attention(qk.reshape(batch, seq, 2 * DF_QK), v.reshape(batch, seq, DF_V),
                               lam_params, diff_norm_w[l].astype(F32)[None, :], gs[:, DN_V:],
                               lambda_init)
        x2d = _out_mlp(x2d, o_dn.reshape(batch * seq, DN_V), o_df.reshape(batch * seq, DF_V),
                       w_out[l].astype(BF16), mlp_norm_w[l][None, :], w_up[l].astype(BF16),
                       w_down[l].astype(BF16), final_norm_w[None, :], l == depth - 1)
    return x2d.reshape(batch, seq, D_MODEL)
```

```python
import functools
import math

import jax
import jax.numpy as jnp
from jax import lax
from jax.experimental import pallas as pl
from jax.experimental.pallas import tpu as pltpu

F32 = jnp.float32
BF16 = jnp.bfloat16

D_MODEL = 1024
DN_HEADS = 4
DN_DK = 128
DN_DV = 128
CONV_WIDTH = 4
CHUNK = 64
DIFF_HEADS = 4
DIFF_DQK = 64
DIFF_DV = 128
ROPE_THETA = 500000.0
ROPE_DIM = 16
D_FF = 4 * D_MODEL
EPS = 1e-6

DN_QK = DN_HEADS * DN_DK
DN_V = DN_HEADS * DN_DV
DN_CONV = 2 * DN_QK + DN_V
DF_QK = DIFF_HEADS * 2 * DIFF_DQK
DF_V = DIFF_HEADS * DIFF_DV
GATE_OFF = DN_CONV + DN_V
N_GATES = 2 * DN_HEADS
MAIN_N = DN_CONV + DN_V + 2 * DF_QK + DF_V
LANES = 128
CONV_PAD = 16

VMEM_LIMIT = 56 * 1024 * 1024
TOKEN_TILE = 512
ATTN_BLOCK = 512
FF_CHUNK = 1024
DN_UNROLL = 2


def _dot(a, b):
    return jnp.dot(a, b, preferred_element_type=F32)


def _dot_nt(a, b):
    return lax.dot_general(a, b, (((1,), (1,)), ((), ())), preferred_element_type=F32)


def _dot_tn(a, b):
    return lax.dot_general(a, b, (((0,), (0,)), ((), ())), preferred_element_type=F32)


def _rms(x, w):
    return x * lax.rsqrt(jnp.mean(x * x, axis=-1, keepdims=True) + EPS) * w


def _silu(x):
    return x / (1.0 + jnp.exp(-x))


def _resident(shape):
    zeros = (0,) * len(shape)
    return pl.BlockSpec(shape, lambda *_: zeros, pipeline_mode=pl.Buffered(1))


def _rope_table_kernel(pos_ref, invf_ref, c_ref, s1_ref, s2_ref):
    ang = pos_ref[...] * invf_ref[...]
    lane = lax.broadcasted_iota(jnp.int32, ang.shape, 1) & (DIFF_DQK - 1)
    c = jnp.cos(ang)
    s = jnp.sin(ang)
    half = ROPE_DIM // 2
    c_ref[...] = jnp.where(lane < ROPE_DIM, c, 1.0)
    s1_ref[...] = jnp.where((lane >= half) & (lane < ROPE_DIM), s, 0.0)
    s2_ref[...] = jnp.where(lane < half, -s, 0.0)


def _rope_tables(positions):
    seq = positions.shape[0]
    half = ROPE_DIM // 2
    inv_freq = ROPE_THETA ** (-jnp.arange(0, ROPE_DIM, 2, dtype=F32) / ROPE_DIM)
    lane = jnp.arange(LANES) % DIFF_DQK
    idx = jnp.where(lane < half, lane, jnp.where(lane < ROPE_DIM, lane - half, 0))
    invf = inv_freq[idx][None, :]
    pos = jnp.broadcast_to(positions.astype(F32)[:, None], (seq, LANES))
    table = jax.ShapeDtypeStruct((seq, LANES), F32)
    return pl.pallas_call(
        _rope_table_kernel,
        out_shape=(table, table, table),
        name="rope_tables",
    )(pos, invf)


def _inproj_kernel(x_ref, nw_ref, w_ref, wg_ref, c_ref, s1_ref, s2_ref, alog_ref, dtb_ref,
                   qkv_ref, z_ref, gate_ref, gate_t_ref, qk_ref, v_ref):
    hn = _rms(x_ref[...], nw_ref[...]).astype(BF16)
    qkv_ref[...] = _dot(hn, w_ref[:, 0:DN_CONV]).astype(BF16)
    z_ref[...] = _dot(hn, w_ref[:, DN_CONV:DN_CONV + DN_V]).astype(BF16)
    off = DN_CONV + DN_V
    qk = _dot(hn, w_ref[:, off:off + 2 * DF_QK])
    c, s1, s2 = c_ref[...], s1_ref[...], s2_ref[...]
    half = ROPE_DIM // 2
    for i in range(2 * DF_QK // LANES):
        blk = qk[:, i * LANES:(i + 1) * LANES]
        rot = (blk * c + pltpu.roll(blk, half, 1) * s1 + pltpu.roll(blk, LANES - half, 1) * s2)
        qk_ref[:, i * LANES:(i + 1) * LANES] = rot.astype(BF16)
    off += 2 * DF_QK
    v_ref[...] = _dot(hn, w_ref[:, off:off + DF_V]).astype(BF16)
    gpre = _dot(hn, wg_ref[...])
    lane = lax.broadcasted_iota(jnp.int32, gpre.shape, 1)
    pos_in_chunk = lax.broadcasted_iota(jnp.int32, gpre.shape, 0) & (CHUNK - 1)
    beta = 1.0 / (1.0 + jnp.exp(-gpre))
    t = gpre + dtb_ref[...]
    softplus = jnp.maximum(t, 0.0) + jnp.log1p(jnp.exp(-jnp.abs(t)))
    gcum = -jnp.exp(alog_ref[...]) * softplus
    shift = 1
    while shift < CHUNK:
        gcum = gcum + jnp.where(pos_in_chunk >= shift, pltpu.roll(gcum, shift, 0), 0.0)
        shift *= 2
    gates = jnp.where(lane < DN_HEADS, beta, gcum)
    gate_ref[...] = gates
    gate_t_ref[...] = gates.T[0:N_GATES, :]


def _input_projection(x2d, seq, norm_w, w_main, w_gate, tables, alog_row, dtb_row):
    tokens = x2d.shape[0]
    tm = min(TOKEN_TILE, seq)
    tiles_per_seq = seq // tm
    row = lambda i: (i, 0)
    pos = lambda i: (i % tiles_per_seq, 0)
    out_shape = (
        jax.ShapeDtypeStruct((tokens, DN_CONV), BF16),
        jax.ShapeDtypeStruct((tokens, DN_V), BF16),
        jax.ShapeDtypeStruct((tokens, LANES), F32),
        jax.ShapeDtypeStruct((N_GATES, tokens), F32),
        jax.ShapeDtypeStruct((tokens, 2 * DF_QK), BF16),
        jax.ShapeDtypeStruct((tokens, DF_V), BF16),
    )
    return pl.pallas_call(
        _inproj_kernel,
        grid=(tokens // tm,),
        in_specs=[
            pl.BlockSpec((tm, D_MODEL), row),
            _resident((1, D_MODEL)),
            _resident((D_MODEL, MAIN_N)),
            _resident((D_MODEL, LANES)),
            pl.BlockSpec((tm, LANES), pos),
            pl.BlockSpec((tm, LANES), pos),
            pl.BlockSpec((tm, LANES), pos),
            _resident((1, LANES)),
            _resident((1, LANES)),
        ],
        out_specs=(
            pl.BlockSpec((tm, DN_CONV), row),
            pl.BlockSpec((tm, DN_V), row),
            pl.BlockSpec((tm, LANES), row),
            pl.BlockSpec((N_GATES, tm), lambda i: (0, i)),
            pl.BlockSpec((tm, 2 * DF_QK), row),
            pl.BlockSpec((tm, DF_V), row),
        ),
        out_shape=out_shape,
        compiler_params=pltpu.CompilerParams(
            dimension_semantics=("arbitrary",), vmem_limit_bytes=VMEM_LIMIT),
        name="input_projection",
    )(x2d, norm_w, w_main, w_gate, *tables, alog_row, dtb_row)


def _deltanet_kernel(qkv_ref, z_ref, gate_ref, gate_t_ref, convw_ref, nw_ref, gs_ref, out_ref,
                     pq_s, n_s, o_s, dec_s, state_ref, *, seq):
    n_chunks = seq // CHUNK
    C = CHUNK
    ii = lax.broadcasted_iota(jnp.int32, (C, C), 0)
    jj = lax.broadcasted_iota(jnp.int32, (C, C), 1)
    eye = (ii == jj).astype(F32)
    lane = lax.broadcasted_iota(jnp.int32, (C, LANES), 1)

    def conv_silu(win, cw):
        y = cw[CONV_WIDTH - 1:CONV_WIDTH, :] * win[CONV_PAD:, :]
        for shift in range(1, CONV_WIDTH):
            tap = CONV_WIDTH - 1 - shift
            y = y + cw[tap:tap + 1, :] * pltpu.roll(win, shift, 0)[CONV_PAD:, :]
        return _silu(y)

    def window(c, col):
        if isinstance(c, int) and c == 0:
            body = qkv_ref[0:C, col:col + LANES].astype(F32)
            return jnp.concatenate([jnp.zeros((CONV_PAD, LANES), F32), body], axis=0)
        start = c * C - CONV_PAD
        if not isinstance(c, int):
            start = pl.multiple_of(start, CONV_PAD)
        return qkv_ref[pl.ds(start, C + CONV_PAD), col:col + LANES].astype(F32)

    def precompute(chains):
        st = []
        for c, h in chains:
            r0 = c * C if isinstance(c, int) else pl.multiple_of(c * C, C)
            rows = pl.ds(r0, C)
            cols = [part * DN_QK + h * LANES for part in range(3)]
            q, k, v = (conv_silu(window(c, col), convw_ref[:, col:col + LANES]) for col in cols)
            qn = q * lax.rsqrt(jnp.sum(q * q, axis=-1, keepdims=True) + 1e-6) * (DN_DK ** -0.5)
            kn = k * lax.rsqrt(jnp.sum(k * k, axis=-1, keepdims=True) + 1e-6)
            gates = gate_ref[rows, :]
            beta = jnp.sum(jnp.where(lane == h, gates, 0.0), axis=-1, keepdims=True)
            g_col = jnp.sum(jnp.where(lane == DN_HEADS + h, gates, 0.0), axis=-1, keepdims=True)
            g_row = gate_t_ref[DN_HEADS + h, pl.ds(c, 1), :]
            dec_incl = jnp.exp(jnp.where(ii >= jj, g_col - g_row, -1e30))
            g_last = g_col[C - 1:C, :]
            eg = jnp.exp(g_col)
            kb = kn * beta
            st.append(dict(
                c=c, h=h, rows=rows, dec_incl=dec_incl, qs=qn * eg, kb=kb, knb=kn.astype(BF16),
                lhs=jnp.concatenate([kb, qn], axis=0).astype(BF16),
                rhs=jnp.concatenate([kb * eg, v * beta], axis=1).astype(BF16),
                kd=(kn * jnp.exp(g_last - g_col)).astype(BF16),
                dec=jnp.broadcast_to(jnp.exp(g_last), (8, LANES))))
        for d in st:
            kq = _dot_nt(d["lhs"], d["knb"])
            d["qk"] = (kq[C:2 * C, :] * d["dec_incl"]).astype(BF16)
            d["n_pow"] = -(kq[0:C, :] * jnp.where(ii > jj, d["dec_incl"], 0.0))
            d["t_inv"] = eye + d["n_pow"]
        for _ in range(int(math.log2(C)) - 1):
            for d in st:
                nb = d["n_pow"].astype(BF16)
                d["n_pow"] = _dot(nb, nb)
            for d in st:
                d["t_inv"] = d["t_inv"] + _dot(d["t_inv"].astype(BF16), d["n_pow"].astype(BF16))
        for d in st:
            d["wu"] = _dot(d["t_inv"].astype(BF16), d["rhs"]).astype(BF16)
        for d in st:
            d["pn"] = _dot_tn(d["kd"], d["wu"])
            d["qo"] = _dot(d["qk"], d["wu"])
        for d in st:
            c, h, pn, qo = d["c"], d["h"], d["pn"], d["qo"]
            pq_s[h, c, 0:DN_DK, :] = pn[:, 0:DN_DV].astype(BF16)
            pq_s[h, c, DN_DK:DN_DK + C, :] = (d["qs"] - qo[:, 0:DN_DV]).astype(BF16)
            n_s[h, c] = pn[:, DN_DV:2 * DN_DV].astype(BF16)
            o_s[h, d["rows"], :] = qo[:, DN_DV:2 * DN_DV]
            dec_s[h, c] = d["dec"]

    precompute([(c, h) for c in range(DN_UNROLL) for h in range(DN_HEADS)])

    def pass1(step, carry):
        precompute([(step * DN_UNROLL + sub, h) for sub in range(DN_UNROLL)
                    for h in range(DN_HEADS)])
        return carry

    lax.fori_loop(1, n_chunks // DN_UNROLL, pass1, 0)

    state_ref[...] = jnp.zeros(state_ref.shape, F32)
    nw = nw_ref[...]

    def recur(c, carry):
        rows = pl.ds(pl.multiple_of(c * C, C), C)
        for h in range(DN_HEADS):
            s = state_ref[h]
            ps = _dot(pq_s[h, c], s.astype(BF16))
            state_ref[h] = s * dec_s[h, c][0:1, :] + n_s[h, c].astype(F32) - ps[0:DN_DK, :]
            o = ps[DN_DK:DN_DK + C, :] + o_s[h, rows, :]
            z = z_ref[rows, h * LANES:(h + 1) * LANES].astype(F32)
            y = _rms(o, nw) * _silu(z) * gs_ref[:, h * LANES:(h + 1) * LANES]
            out_ref[rows, h * LANES:(h + 1) * LANES] = y.astype(BF16)
        return carry

    lax.fori_loop(0, n_chunks, recur, 0)


def _deltanet(qkv, z, gates, gates_t, conv_w, norm_w, gs):
    batch, seq, _ = qkv.shape
    n_chunks = seq // CHUNK
    assert n_chunks % DN_UNROLL == 0
    per_batch = lambda width: pl.BlockSpec((None, seq, width), lambda b: (b, 0, 0))
    return pl.pallas_call(
        functools.partial(_deltanet_kernel, seq=seq),
        grid=(batch,),
        in_specs=[
            per_batch(DN_CONV),
            per_batch(DN_V),
            per_batch(LANES),
            pl.BlockSpec((N_GATES, n_chunks, CHUNK), lambda b: (0, b, 0)),
            _resident((CONV_WIDTH, DN_CONV)),
            _resident((1, DN_DV)),
            _resident((1, DN_V)),
        ],
        out_specs=per_batch(DN_V),
        out_shape=jax.ShapeDtypeStruct((batch, seq, DN_V), BF16),
        scratch_shapes=[
            pltpu.VMEM((DN_HEADS, n_chunks, DN_DK + CHUNK, DN_DV), BF16),
            pltpu.VMEM((DN_HEADS, n_chunks, DN_DK, DN_DV), BF16),
            pltpu.VMEM((DN_HEADS, seq, DN_DV), F32),
            pltpu.VMEM((DN_HEADS, n_chunks, 8, LANES), F32),
            pltpu.VMEM((DN_HEADS, DN_DK, DN_DV), F32),
        ],
        compiler_params=pltpu.CompilerParams(
            dimension_semantics=("arbitrary",), vmem_limit_bytes=VMEM_LIMIT),
        name="gated_deltanet",
    )(qkv, z, gates, gates_t, conv_w, norm_w, gs)


def _diff_attn_kernel(lam_ref, nw_ref, gs_ref, q_ref, k_ref, v_ref, o_ref, *, seq, block,
                      lambda_init):
    lp = lam_ref[...]
    lam = (jnp.exp(jnp.sum(lp[0:1, :] * lp[1:2, :], axis=-1, keepdims=True))
           - jnp.exp(jnp.sum(lp[2:3, :] * lp[3:4, :], axis=-1, keepdims=True)) + lambda_init)
    lane = lax.broadcasted_iota(jnp.int32, (block, LANES), 1)
    first_map = lane < DIFF_DQK
    row = lax.broadcasted_iota(jnp.int32, (block, block), 0)
    col = lax.broadcasted_iota(jnp.int32, (block, block), 1)
    causal = row >= col
    scale = DIFF_DQK ** -0.5
    zero = jnp.zeros((), BF16)

    def update(carry, q1, q2, kb, vb, mask):
        new = []
        for qm, (m, l, acc) in zip((q1, q2), carry):
            s = _dot_nt(qm, kb)
            if mask is not None:
                s = jnp.where(mask, s, -1e30)
            m_new = jnp.maximum(m, jnp.max(s, axis=-1, keepdims=True))
            alpha = jnp.exp(m - m_new)
            p = jnp.exp(s - m_new)
            l_new = alpha * l + jnp.sum(p, axis=-1, keepdims=True)
            acc_new = alpha * acc + _dot(p.astype(BF16), vb)
            new.append((m_new, l_new, acc_new))
        return tuple(new)

    for qi in range(seq // block):
        q = q_ref[qi * block:(qi + 1) * block, :] * jnp.asarray(scale, BF16)
        q1 = jnp.where(first_map, q, zero)
        q2 = jnp.where(first_map, zero, q)
        init_map = (jnp.full((block, 1), -1e30, F32), jnp.zeros((block, 1), F32),
                    jnp.zeros((block, DIFF_DV), F32))

        def body(j, carry, q1=q1, q2=q2):
            r0 = pl.multiple_of(j * block, block)
            return update(carry, q1, q2, k_ref[pl.ds(r0, block), :], v_ref[pl.ds(r0, block), :],
                          None)

        carry = lax.fori_loop(0, qi, body, (init_map, init_map))
        rows = slice(qi * block, (qi + 1) * block)
        (_, l1, a1), (_, l2, a2) = update(carry, q1, q2, k_ref[rows, :], v_ref[rows, :], causal)
        o = a1 / l1 - lam * (a2 / l2)
        y = _rms(o, nw_ref[...]) * (1.0 - lambda_init) * gs_ref[...]
        o_ref[rows, :] = y.astype(BF16)


def _diff_attention(qk, v, lam_params, norm_w, gs, lambda_init):
    batch, seq, _ = v.shape
    block = min(ATTN_BLOCK, seq)
    head_cols = lambda off: pl.BlockSpec((None, seq, LANES), lambda b, h: (b, 0, off + h))
    return pl.pallas_call(
        functools.partial(_diff_attn_kernel, seq=seq, block=block, lambda_init=lambda_init),
        grid=(batch, DIFF_HEADS),
        in_specs=[
            _resident((4, DIFF_DQK)),
            _resident((1, DIFF_DV)),
            pl.BlockSpec((1, LANES), lambda b, h: (0, h)),
            head_cols(0),
            head_cols(DIFF_HEADS),
            head_cols(0),
        ],
        out_specs=head_cols(0),
        out_shape=jax.ShapeDtypeStruct((batch, seq, DF_V), BF16),
        compiler_params=pltpu.CompilerParams(
            dimension_semantics=("arbitrary", "arbitrary"), vmem_limit_bytes=VMEM_LIMIT),
        name="diff_attention",
    )(lam_params, norm_w, gs, qk, qk, v)


def _out_mlp_kernel(x_ref, odn_ref, odf_ref, wo_ref, nw_ref, wu_ref, wd_ref, fw_ref, y_ref,
                    hidden_ref, *, final_norm):
    x1 = (x_ref[...] + _dot(odn_ref[...], wo_ref[0:DN_V, :])
          + _dot(odf_ref[...], wo_ref[DN_V:DN_V + DF_V, :]))
    hm = _rms(x1, nw_ref[...]).astype(BF16)
    for c in range(D_FF // FF_CHUNK):
        cols = slice(c * FF_CHUNK, (c + 1) * FF_CHUNK)
        hidden_ref[:, cols] = jnp.square(jnp.maximum(_dot(hm, wu_ref[:, cols]), 0.0)).astype(BF16)
    y = x1 + _dot(hidden_ref[...], wd_ref[...])
    if final_norm:
        y = _rms(y, fw_ref[...])
    y_ref[...] = y


def _out_mlp(x2d, o_dn, o_df, w_out, norm_w, w_up, w_down, final_w, final_norm):
    tokens = x2d.shape[0]
    tm = min(TOKEN_TILE, tokens)
    row = lambda i: (i, 0)
    return pl.pallas_call(
        functools.partial(_out_mlp_kernel, final_norm=final_norm),
        grid=(tokens // tm,),
        in_specs=[
            pl.BlockSpec((tm, D_MODEL), row),
            pl.BlockSpec((tm, DN_V), row),
            pl.BlockSpec((tm, DF_V), row),
            _resident((DN_V + DF_V, D_MODEL)),
            _resident((1, D_MODEL)),
            _resident((D_MODEL, D_FF)),
            _resident((D_FF, D_MODEL)),
            _resident((1, D_MODEL)),
        ],
        out_specs=pl.BlockSpec((tm, D_MODEL), row),
        out_shape=jax.ShapeDtypeStruct((tokens, D_MODEL), F32),
        scratch_shapes=[pltpu.VMEM((tm, D_FF), BF16)],
        compiler_params=pltpu.CompilerParams(
            dimension_semantics=("arbitrary",), vmem_limit_bytes=VMEM_LIMIT),
        name="out_projection_mlp",
    )(x2d, o_dn, o_df, w_out, norm_w, w_up, w_down, final_w)


def _pad_lanes(vec, offset):
    return jnp.zeros((1, LANES), F32).at[0, offset:offset + vec.shape[0]].set(vec.astype(F32))


def kernel(x, positions, attn_norm_w, w_in, conv_w, a_log, dt_bias, dn_norm_w, lambda_q1, lambda_k1, lambda_q2, lambda_k2, diff_norm_w, group_scale, w_out, mlp_norm_w, w_up, w_down, final_norm_w):
    batch, seq, _ = x.shape
    depth = w_in.shape[0]
    tables = _rope_tables(positions)
    x2d = x.reshape(batch * seq, D_MODEL)
    for l in range(depth):
        wl = w_in[l]
        w_main = jnp.concatenate([wl[:, :GATE_OFF], wl[:, GATE_OFF + N_GATES:]], axis=1).astype(BF16)
        w_gate = jnp.pad(wl[:, GATE_OFF:GATE_OFF + N_GATES], ((0, 0), (0, LANES - N_GATES))).astype(BF16)
        qkv, z, gates, gates_t, qk, v = _input_projection(
            x2d, seq, attn_norm_w[l][None, :], w_main, w_gate, tables,
            _pad_lanes(a_log[l], DN_HEADS), _pad_lanes(dt_bias[l], DN_HEADS))
        gs = group_scale[l].astype(F32)[None, :]
        o_dn = _deltanet(qkv.reshape(batch, seq, DN_CONV), z.reshape(batch, seq, DN_V),
                         gates.reshape(batch, seq, LANES),
                         gates_t.reshape(N_GATES, batch * seq // CHUNK, CHUNK), conv_w[l].astype(F32),
                         dn_norm_w[l].astype(F32)[None, :], gs[:, :DN_V])
        lam_params = jnp.stack([lambda_q1[l], lambda_k1[l], lambda_q2[l], lambda_k2[l]]).astype(F32)
        lambda_init = 0.8 - 0.6 * math.exp(-0.3 * l)
        o_df = _diff_attention(qk.reshape(batch, seq, 2 * DF_QK), v.reshape(batch, seq, DF_V),
                               lam_params, diff_norm_w[l].astype(F32)[None, :], gs[:, DN_V:],
                               lambda_init)
        x2d = _out_mlp(x2d, o_dn.reshape(batch * seq, DN_V), o_df.reshape(batch * seq, DF_V),
                       w_out[l].astype(BF16), mlp_norm_w[l][None, :], w_up[l].astype(BF16),
                       w_down[l].astype(BF16), final_norm_w[None, :], l == depth - 1)
    return x2d.reshape(batch, seq, D_MODEL)
```

```python
import functools
import math

import jax
import jax.numpy as jnp
from jax import lax
from jax.experimental import pallas as pl
from jax.experimental.pallas import tpu as pltpu

F32 = jnp.float32
BF16 = jnp.bfloat16

D_MODEL = 1024
DN_HEADS = 4
DN_DK = 128
DN_DV = 128
CONV_WIDTH = 4
CHUNK = 64
DIFF_HEADS = 4
DIFF_DQK = 64
DIFF_DV = 128
ROPE_THETA = 500000.0
ROPE_DIM = 16
D_FF = 4 * D_MODEL
EPS = 1e-6

DN_QK = DN_HEADS * DN_DK
DN_V = DN_HEADS * DN_DV
DN_CONV = 2 * DN_QK + DN_V
DF_QK = DIFF_HEADS * 2 * DIFF_DQK
DF_V = DIFF_HEADS * DIFF_DV
GATE_OFF = DN_CONV + DN_V
N_GATES = 2 * DN_HEADS
MAIN_N = DN_CONV + DN_V + 2 * DF_QK + DF_V
LANES = 128
CONV_CARRY = 8
CONV_ROWS = 64

VMEM_LIMIT = 56 * 1024 * 1024
TOKEN_TILE = 512
ATTN_BLOCK = 512
FF_CHUNK = 1024
DN_UNROLL = 8


def _dot(a, b):
    return jnp.dot(a, b, preferred_element_type=F32)


def _dot_nt(a, b):
    return lax.dot_general(a, b, (((1,), (1,)), ((), ())), preferred_element_type=F32)


def _dot_tn(a, b):
    return lax.dot_general(a, b, (((0,), (0,)), ((), ())), preferred_element_type=F32)


def _rms(x, w):
    return x * lax.rsqrt(jnp.mean(x * x, axis=-1, keepdims=True) + EPS) * w


def _silu(x):
    return x / (1.0 + jnp.exp(-x))


def _resident(shape):
    zeros = (0,) * len(shape)
    return pl.BlockSpec(shape, lambda *_: zeros, pipeline_mode=pl.Buffered(1))


def _rope_table_kernel(pos_ref, invf_ref, c_ref, s1_ref, s2_ref):
    ang = pos_ref[...] * invf_ref[...]
    lane = lax.broadcasted_iota(jnp.int32, ang.shape, 1) & (DIFF_DQK - 1)
    c = jnp.cos(ang)
    s = jnp.sin(ang)
    half = ROPE_DIM // 2
    c_ref[...] = jnp.where(lane < ROPE_DIM, c, 1.0)
    s1_ref[...] = jnp.where((lane >= half) & (lane < ROPE_DIM), s, 0.0)
    s2_ref[...] = jnp.where(lane < half, -s, 0.0)


def _rope_tables(positions):
    seq = positions.shape[0]
    half = ROPE_DIM // 2
    inv_freq = ROPE_THETA ** (-jnp.arange(0, ROPE_DIM, 2, dtype=F32) / ROPE_DIM)
    lane = jnp.arange(LANES) % DIFF_DQK
    idx = jnp.where(lane < half, lane, jnp.where(lane < ROPE_DIM, lane - half, 0))
    invf = inv_freq[idx][None, :]
    pos = jnp.broadcast_to(positions.astype(F32)[:, None], (seq, LANES))
    table = jax.ShapeDtypeStruct((seq, LANES), F32)
    return pl.pallas_call(
        _rope_table_kernel,
        out_shape=(table, table, table),
        name="rope_tables",
    )(pos, invf)


def _inproj_kernel(x_ref, nw_ref, w_ref, wg_ref, c_ref, s1_ref, s2_ref, alog_ref, dtb_ref,
                   convw_ref, qkv_ref, z_ref, gate_ref, gate_t_ref, qk_ref, v_ref, pre_ref, *,
                   tiles_per_seq):
    hn = _rms(x_ref[...], nw_ref[...]).astype(BF16)
    tm = x_ref.shape[0]

    @pl.when((pl.program_id(0) % tiles_per_seq) == 0)
    def _():
        pre_ref[0:CONV_CARRY, :] = jnp.zeros((CONV_CARRY, DN_CONV), F32)

    pre_ref[CONV_CARRY:CONV_CARRY + tm, :] = _dot(hn, w_ref[:, 0:DN_CONV])
    for i in range(DN_CONV // LANES):
        cols = slice(i * LANES, (i + 1) * LANES)
        cw = convw_ref[:, cols]
        for r0 in range(0, tm, CONV_ROWS):
            win = pre_ref[r0:r0 + CONV_CARRY + CONV_ROWS, cols]
            y = cw[CONV_WIDTH - 1:CONV_WIDTH, :] * win[CONV_CARRY:, :]
            for shift in range(1, CONV_WIDTH):
                tap = CONV_WIDTH - 1 - shift
                y = y + cw[tap:tap + 1, :] * pltpu.roll(win, shift, 0)[CONV_CARRY:, :]
            y = _silu(y)
            if i < 2 * DN_HEADS:
                scale = DN_DK ** -0.5 if i < DN_HEADS else 1.0
                y = y * (lax.rsqrt(jnp.sum(y * y, axis=-1, keepdims=True) + 1e-6) * scale)
            qkv_ref[r0:r0 + CONV_ROWS, cols] = y.astype(BF16)
    pre_ref[0:CONV_CARRY, :] = pre_ref[tm:tm + CONV_CARRY, :]
    z_ref[...] = _dot(hn, w_ref[:, DN_CONV:DN_CONV + DN_V]).astype(BF16)
    off = DN_CONV + DN_V
    qk = _dot(hn, w_ref[:, off:off + 2 * DF_QK])
    c, s1, s2 = c_ref[...], s1_ref[...], s2_ref[...]
    half = ROPE_DIM // 2
    for i in range(2 * DF_QK // LANES):
        blk = qk[:, i * LANES:(i + 1) * LANES]
        rot = (blk * c + pltpu.roll(blk, half, 1) * s1 + pltpu.roll(blk, LANES - half, 1) * s2)
        qk_ref[:, i * LANES:(i + 1) * LANES] = rot.astype(BF16)
    off += 2 * DF_QK
    v_ref[...] = _dot(hn, w_ref[:, off:off + DF_V]).astype(BF16)
    gpre = _dot(hn, wg_ref[...])
    lane = lax.broadcasted_iota(jnp.int32, gpre.shape, 1)
    pos_in_chunk = lax.broadcasted_iota(jnp.int32, gpre.shape, 0) & (CHUNK - 1)
    beta = 1.0 / (1.0 + jnp.exp(-gpre))
    t = gpre + dtb_ref[...]
    softplus = jnp.maximum(t, 0.0) + jnp.log1p(jnp.exp(-jnp.abs(t)))
    gcum = -jnp.exp(alog_ref[...]) * softplus
    shift = 1
    while shift < CHUNK:
        gcum = gcum + jnp.where(pos_in_chunk >= shift, pltpu.roll(gcum, shift, 0), 0.0)
        shift *= 2
    gates = jnp.where(lane < DN_HEADS, beta, gcum)
    gate_ref[...] = gates
    gate_t_ref[...] = gates.T[0:N_GATES, :]


def _input_projection(x2d, seq, norm_w, w_main, w_gate, tables, alog_row, dtb_row, conv_w):
    tokens = x2d.shape[0]
    tm = min(TOKEN_TILE, seq)
    tiles_per_seq = seq // tm
    row = lambda i: (i, 0)
    pos = lambda i: (i % tiles_per_seq, 0)
    out_shape = (
        jax.ShapeDtypeStruct((tokens, DN_CONV), BF16),
        jax.ShapeDtypeStruct((tokens, DN_V), BF16),
        jax.ShapeDtypeStruct((tokens, LANES), F32),
        jax.ShapeDtypeStruct((N_GATES, tokens), F32),
        jax.ShapeDtypeStruct((tokens, 2 * DF_QK), BF16),
        jax.ShapeDtypeStruct((tokens, DF_V), BF16),
    )
    return pl.pallas_call(
        functools.partial(_inproj_kernel, tiles_per_seq=tiles_per_seq),
        grid=(tokens // tm,),
        in_specs=[
            pl.BlockSpec((tm, D_MODEL), row),
            _resident((1, D_MODEL)),
            _resident((D_MODEL, MAIN_N)),
            _resident((D_MODEL, LANES)),
            pl.BlockSpec((tm, LANES), pos),
            pl.BlockSpec((tm, LANES), pos),
            pl.BlockSpec((tm, LANES), pos),
            _resident((1, LANES)),
            _resident((1, LANES)),
            _resident((CONV_WIDTH, DN_CONV)),
        ],
        out_specs=(
            pl.BlockSpec((tm, DN_CONV), row),
            pl.BlockSpec((tm, DN_V), row),
            pl.BlockSpec((tm, LANES), row),
            pl.BlockSpec((N_GATES, tm), lambda i: (0, i)),
            pl.BlockSpec((tm, 2 * DF_QK), row),
            pl.BlockSpec((tm, DF_V), row),
        ),
        out_shape=out_shape,
        scratch_shapes=[pltpu.VMEM((CONV_CARRY + tm, DN_CONV), F32)],
        compiler_params=pltpu.CompilerParams(
            dimension_semantics=("arbitrary",), vmem_limit_bytes=VMEM_LIMIT),
        name="input_projection",
    )(x2d, norm_w, w_main, w_gate, *tables, alog_row, dtb_row, conv_w)


def _deltanet_kernel(qkv_ref, z_ref, gate_ref, gate_t_ref, nw_ref, gs_ref, out_ref,
                     pq_s, n_s, o_s, dec_s, state_ref, *, seq):
    n_chunks = seq // CHUNK
    C = CHUNK
    ii = lax.broadcasted_iota(jnp.int32, (C, C), 0)
    jj = lax.broadcasted_iota(jnp.int32, (C, C), 1)
    eye = (ii == jj).astype(F32)
    lane = lax.broadcasted_iota(jnp.int32, (C, LANES), 1)

    def precompute(chains):
        st = []
        for c, h in chains:
            r0 = c * C if isinstance(c, int) else pl.multiple_of(c * C, C)
            rows = pl.ds(r0, C)
            qn, kn, v = (qkv_ref[rows, part * DN_QK + h * LANES:part * DN_QK + (h + 1) * LANES]
                         .astype(F32) for part in range(3))
            gates = gate_ref[rows, :]
            beta = jnp.sum(jnp.where(lane == h, gates, 0.0), axis=-1, keepdims=True)
            g_col = jnp.sum(jnp.where(lane == DN_HEADS + h, gates, 0.0), axis=-1, keepdims=True)
            g_row = gate_t_ref[DN_HEADS + h, pl.ds(c, 1), :]
            dec_incl = jnp.exp(jnp.where(ii >= jj, g_col - g_row, -1e30))
            g_last = g_col[C - 1:C, :]
            eg = jnp.exp(g_col)
            kb = kn * beta
            st.append(dict(
                c=c, h=h, rows=rows, dec_incl=dec_incl, qs=qn * eg, kb=kb, knb=kn.astype(BF16),
                lhs=jnp.concatenate([kb, qn], axis=0).astype(BF16),
                rhs=jnp.concatenate([kb * eg, v * beta], axis=1).astype(BF16),
                kd=(kn * jnp.exp(g_last - g_col)).astype(BF16),
                dec=jnp.broadcast_to(jnp.exp(g_last), (8, LANES))))
        for d in st:
            kq = _dot_nt(d["lhs"], d["knb"])
            d["qk"] = (kq[C:2 * C, :] * d["dec_incl"]).astype(BF16)
            d["n_pow"] = -(kq[0:C, :] * jnp.where(ii > jj, d["dec_incl"], 0.0))
            d["t_inv"] = eye + d["n_pow"]
        for _ in range(int(math.log2(C)) - 1):
            for d in st:
                nb = d["n_pow"].astype(BF16)
                d["n_pow"] = _dot(nb, nb)
            for d in st:
                d["t_inv"] = d["t_inv"] + _dot(d["t_inv"].astype(BF16), d["n_pow"].astype(BF16))
        for d in st:
            d["wu"] = _dot(d["t_inv"].astype(BF16), d["rhs"]).astype(BF16)
        for d in st:
            d["pn"] = _dot_tn(d["kd"], d["wu"])
            d["qo"] = _dot(d["qk"], d["wu"])
        for d in st:
            c, h, pn, qo = d["c"], d["h"], d["pn"], d["qo"]
            pq_s[h, c, 0:DN_DK, :] = pn[:, 0:DN_DV].astype(BF16)
            pq_s[h, c, DN_DK:DN_DK + C, :] = (d["qs"] - qo[:, 0:DN_DV]).astype(BF16)
            n_s[h, c] = pn[:, DN_DV:2 * DN_DV].astype(BF16)
            o_s[h, d["rows"], :] = qo[:, DN_DV:2 * DN_DV]
            dec_s[h, c] = d["dec"]

    nw = nw_ref[...]

    def recur(group):
        states = [state_ref[h] for h in range(DN_HEADS)]
        for sub in range(DN_UNROLL):
            c = group * DN_UNROLL + sub
            r0 = c * C if isinstance(c, int) else pl.multiple_of(c * C, C)
            rows = pl.ds(r0, C)
            for h in range(DN_HEADS):
                s = states[h]
                ps = _dot(pq_s[h, c], s.astype(BF16))
                states[h] = s * dec_s[h, c][0:1, :] + n_s[h, c].astype(F32) - ps[0:DN_DK, :]
                o = ps[DN_DK:DN_DK + C, :] + o_s[h, rows, :]
                z = z_ref[rows, h * LANES:(h + 1) * LANES].astype(F32)
                y = _rms(o, nw) * _silu(z) * gs_ref[:, h * LANES:(h + 1) * LANES]
                out_ref[rows, h * LANES:(h + 1) * LANES] = y.astype(BF16)
        for h in range(DN_HEADS):
            state_ref[h] = states[h]

    def group_chains(group):
        return [(group * DN_UNROLL + sub, h) for sub in range(DN_UNROLL) for h in range(DN_HEADS)]

    n_groups = n_chunks // DN_UNROLL
    state_ref[...] = jnp.zeros(state_ref.shape, F32)
    precompute(group_chains(0))

    def step(t, carry):
        recur(t - 1)
        precompute(group_chains(t))
        return carry

    lax.fori_loop(1, n_groups, step, 0)
    recur(n_groups - 1)


def _deltanet(qkv, z, gates, gates_t, norm_w, gs):
    batch, seq, _ = qkv.shape
    n_chunks = seq // CHUNK
    assert n_chunks % DN_UNROLL == 0
    per_batch = lambda width: pl.BlockSpec((None, seq, width), lambda b: (b, 0, 0))
    return pl.pallas_call(
        functools.partial(_deltanet_kernel, seq=seq),
        grid=(batch,),
        in_specs=[
            per_batch(DN_CONV),
            per_batch(DN_V),
            per_batch(LANES),
            pl.BlockSpec((N_GATES, n_chunks, CHUNK), lambda b: (0, b, 0)),
            _resident((1, DN_DV)),
            _resident((1, DN_V)),
        ],
        out_specs=per_batch(DN_V),
        out_shape=jax.ShapeDtypeStruct((batch, seq, DN_V), BF16),
        scratch_shapes=[
            pltpu.VMEM((DN_HEADS, n_chunks, DN_DK + CHUNK, DN_DV), BF16),
            pltpu.VMEM((DN_HEADS, n_chunks, DN_DK, DN_DV), BF16),
            pltpu.VMEM((DN_HEADS, seq, DN_DV), F32),
            pltpu.VMEM((DN_HEADS, n_chunks, 8, LANES), F32),
            pltpu.VMEM((DN_HEADS, DN_DK, DN_DV), F32),
        ],
        compiler_params=pltpu.CompilerParams(
            dimension_semantics=("arbitrary",), vmem_limit_bytes=VMEM_LIMIT),
        name="gated_deltanet",
    )(qkv, z, gates, gates_t, norm_w, gs)


def _diff_attn_kernel(lam_ref, nw_ref, gs_ref, q_ref, k_ref, v_ref, o_ref, *, seq, block,
                      lambda_init):
    lp = lam_ref[...]
    lam = (jnp.exp(jnp.sum(lp[0:1, :] * lp[1:2, :], axis=-1, keepdims=True))
           - jnp.exp(jnp.sum(lp[2:3, :] * lp[3:4, :], axis=-1, keepdims=True)) + lambda_init)
    lane = lax.broadcasted_iota(jnp.int32, (block, LANES), 1)
    first_map = lane < DIFF_DQK
    row = lax.broadcasted_iota(jnp.int32, (block, block), 0)
    col = lax.broadcasted_iota(jnp.int32, (block, block), 1)
    causal = row >= col
    scale = DIFF_DQK ** -0.5
    zero = jnp.zeros((), BF16)

    def update(carry, q1, q2, kb, vb, mask):
        new = []
        for qm, (m, l, acc) in zip((q1, q2), carry):
            s = _dot_nt(qm, kb)
            if mask is not None:
                s = jnp.where(mask, s, -1e30)
            m_new = jnp.maximum(m, jnp.max(s, axis=-1, keepdims=True))
            alpha = jnp.exp(m - m_new)
            p = jnp.exp(s - m_new)
            l_new = alpha * l + jnp.sum(p, axis=-1, keepdims=True)
            acc_new = alpha * acc + _dot(p.astype(BF16), vb)
            new.append((m_new, l_new, acc_new))
        return tuple(new)

    for qi in range(seq // block):
        q = q_ref[qi * block:(qi + 1) * block, :] * jnp.asarray(scale, BF16)
        q1 = jnp.where(first_map, q, zero)
        q2 = jnp.where(first_map, zero, q)
        init_map = (jnp.full((block, 1), -1e30, F32), jnp.zeros((block, 1), F32),
                    jnp.zeros((block, DIFF_DV), F32))

        def body(j, carry, q1=q1, q2=q2):
            r0 = pl.multiple_of(j * block, block)
            return update(carry, q1, q2, k_ref[pl.ds(r0, block), :], v_ref[pl.ds(r0, block), :],
                          None)

        carry = lax.fori_loop(0, qi, body, (init_map, init_map))
        rows = slice(qi * block, (qi + 1) * block)
        (_, l1, a1), (_, l2, a2) = update(carry, q1, q2, k_ref[rows, :], v_ref[rows, :], causal)
        o = a1 / l1 - lam * (a2 / l2)
        y = _rms(o, nw_ref[...]) * (1.0 - lambda_init) * gs_ref[...]
        o_ref[rows, :] = y.astype(BF16)


def _diff_attention(qk, v, lam_params, norm_w, gs, lambda_init):
    batch, seq, _ = v.shape
    block = min(ATTN_BLOCK, seq)
    head_cols = lambda off: pl.BlockSpec((None, seq, LANES), lambda b, h: (b, 0, off + h))
    return pl.pallas_call(
        functools.partial(_diff_attn_kernel, seq=seq, block=block, lambda_init=lambda_init),
        grid=(batch, DIFF_HEADS),
        in_specs=[
            _resident((4, DIFF_DQK)),
            _resident((1, DIFF_DV)),
            pl.BlockSpec((1, LANES), lambda b, h: (0, h)),
            head_cols(0),
            head_cols(DIFF_HEADS),
            head_cols(0),
        ],
        out_specs=head_cols(0),
        out_shape=jax.ShapeDtypeStruct((batch, seq, DF_V), BF16),
        compiler_params=pltpu.CompilerParams(
            dimension_semantics=("arbitrary", "arbitrary"), vmem_limit_bytes=VMEM_LIMIT),
        name="diff_attention",
    )(lam_params, norm_w, gs, qk, qk, v)


def _out_mlp_kernel(x_ref, odn_ref, odf_ref, wo_ref, nw_ref, wu_ref, wd_ref, fw_ref, y_ref,
                    hidden_ref, *, final_norm):
    x1 = (x_ref[...] + _dot(odn_ref[...], wo_ref[0:DN_V, :])
          + _dot(odf_ref[...], wo_ref[DN_V:DN_V + DF_V, :]))
    hm = _rms(x1, nw_ref[...]).astype(BF16)
    for c in range(D_FF // FF_CHUNK):
        cols = slice(c * FF_CHUNK, (c + 1) * FF_CHUNK)
        hidden_ref[:, cols] = jnp.square(jnp.maximum(_dot(hm, wu_ref[:, cols]), 0.0)).astype(BF16)
    y = x1 + _dot(hidden_ref[...], wd_ref[...])
    if final_norm:
        y = _rms(y, fw_ref[...])
    y_ref[...] = y


def _out_mlp(x2d, o_dn, o_df, w_out, norm_w, w_up, w_down, final_w, final_norm):
    tokens = x2d.shape[0]
    tm = min(TOKEN_TILE, tokens)
    row = lambda i: (i, 0)
    return pl.pallas_call(
        functools.partial(_out_mlp_kernel, final_norm=final_norm),
        grid=(tokens // tm,),
        in_specs=[
            pl.BlockSpec((tm, D_MODEL), row),
            pl.BlockSpec((tm, DN_V), row),
            pl.BlockSpec((tm, DF_V), row),
            _resident((DN_V + DF_V, D_MODEL)),
            _resident((1, D_MODEL)),
            _resident((D_MODEL, D_FF)),
            _resident((D_FF, D_MODEL)),
            _resident((1, D_MODEL)),
        ],
        out_specs=pl.BlockSpec((tm, D_MODEL), row),
        out_shape=jax.ShapeDtypeStruct((tokens, D_MODEL), F32),
        scratch_shapes=[pltpu.VMEM((tm, D_FF), BF16)],
        compiler_params=pltpu.CompilerParams(
            dimension_semantics=("arbitrary",), vmem_limit_bytes=VMEM_LIMIT),
        name="out_projection_mlp",
    )(x2d, o_dn, o_df, w_out, norm_w, w_up, w_down, final_w)


def _pad_lanes(vec, offset):
    return jnp.zeros((1, LANES), F32).at[0, offset:offset + vec.shape[0]].set(vec.astype(F32))


def kernel(x, positions, attn_norm_w, w_in, conv_w, a_log, dt_bias, dn_norm_w, lambda_q1, lambda_k1, lambda_q2, lambda_k2, diff_norm_w, group_scale, w_out, mlp_norm_w, w_up, w_down, final_norm_w):
    batch, seq, _ = x.shape
    depth = w_in.shape[0]
    tables = _rope_tables(positions)
    x2d = x.reshape(batch * seq, D_MODEL)
    for l in range(depth):
        wl = w_in[l]
        w_main = jnp.concatenate([wl[:, :GATE_OFF], wl[:, GATE_OFF + N_GATES:]], axis=1).astype(BF16)
        w_gate = jnp.pad(wl[:, GATE_OFF:GATE_OFF + N_GATES], ((0, 0), (0, LANES - N_GATES))).astype(BF16)
        qkv, z, gates, gates_t, qk, v = _input_projection(
            x2d, seq, attn_norm_w[l][None, :], w_main, w_gate, tables,
            _pad_lanes(a_log[l], DN_HEADS), _pad_lanes(dt_bias[l], DN_HEADS), conv_w[l].astype(F32))
        gs = group_scale[l].astype(F32)[None, :]
        o_dn = _deltanet(qkv.reshape(batch, seq, DN_CONV), z.reshape(batch, seq, DN_V),
                         gates.reshape(batch, seq, LANES),
                         gates_t.reshape(N_GATES, batch * seq // CHUNK, CHUNK),
                         dn_norm_w[l].astype(F32)[None, :], gs[:, :DN_V])
        lam_params = jnp.stack([lambda_q1[l], lambda_k1[l], lambda_q2[l], lambda_k2[l]]).astype(F32)
        lambda_init = 0.8 - 0.6 * math.exp(-0.3 * l)
        o_df = _diff_attention(qk.reshape(batch, seq, 2 * DF_QK), v.reshape(batch, seq, DF_V),
                               lam_params, diff_norm_w[l].astype(F32)[None, :], gs[:, DN_V:],
                               lambda_init)
        x2d = _out_mlp(x2d, o_dn.reshape(batch * seq, DN_V), o_df.reshape(batch * seq, DF_V),
                       w_out[l].astype(BF16), mlp_norm_w[l][None, :], w_up[l].astype(BF16),
                       w_down[l].astype(BF16), final_norm_w[None, :], l == depth - 1)
    return x2d.reshape(batch, seq, D_MODEL)
```

```python
import functools
import math

import jax
import jax.numpy as jnp
from jax import lax
from jax.experimental import pallas as pl
from jax.experimental.pallas import tpu as pltpu

F32 = jnp.float32
BF16 = jnp.bfloat16

D_MODEL = 1024
DN_HEADS = 4
DN_DK = 128
DN_DV = 128
CONV_WIDTH = 4
CHUNK = 64
DIFF_HEADS = 4
DIFF_DQK = 64
DIFF_DV = 128
ROPE_THETA = 500000.0
ROPE_DIM = 16
D_FF = 4 * D_MODEL
EPS = 1e-6

DN_QK = DN_HEADS * DN_DK
DN_V = DN_HEADS * DN_DV
DN_CONV = 2 * DN_QK + DN_V
DF_QK = DIFF_HEADS * 2 * DIFF_DQK
DF_V = DIFF_HEADS * DIFF_DV
GATE_OFF = DN_CONV + DN_V
N_GATES = 2 * DN_HEADS
MAIN_N = DN_CONV + DN_V + 2 * DF_QK + DF_V
LANES = 128
CONV_CARRY = 8
CONV_ROWS = 64

VMEM_LIMIT = 56 * 1024 * 1024
TOKEN_TILE = 512
ATTN_BLOCK = 512
FF_CHUNK = 1024
DN_UNROLL = 8


def _dot(a, b):
    return jnp.dot(a, b, preferred_element_type=F32)


def _dot_nt(a, b):
    return lax.dot_general(a, b, (((1,), (1,)), ((), ())), preferred_element_type=F32)


def _dot_tn(a, b):
    return lax.dot_general(a, b, (((0,), (0,)), ((), ())), preferred_element_type=F32)


def _rms(x, w):
    return x * lax.rsqrt(jnp.mean(x * x, axis=-1, keepdims=True) + EPS) * w


def _silu(x):
    return x / (1.0 + jnp.exp(-x))


def _resident(shape):
    zeros = (0,) * len(shape)
    return pl.BlockSpec(shape, lambda *_: zeros, pipeline_mode=pl.Buffered(1))


def _rope_table_kernel(pos_ref, invf_ref, c_ref, s1_ref, s2_ref):
    ang = pos_ref[...] * invf_ref[...]
    lane = lax.broadcasted_iota(jnp.int32, ang.shape, 1) & (DIFF_DQK - 1)
    c = jnp.cos(ang)
    s = jnp.sin(ang)
    half = ROPE_DIM // 2
    c_ref[...] = jnp.where(lane < ROPE_DIM, c, 1.0)
    s1_ref[...] = jnp.where((lane >= half) & (lane < ROPE_DIM), s, 0.0)
    s2_ref[...] = jnp.where(lane < half, -s, 0.0)


def _rope_tables(positions):
    seq = positions.shape[0]
    half = ROPE_DIM // 2
    inv_freq = ROPE_THETA ** (-jnp.arange(0, ROPE_DIM, 2, dtype=F32) / ROPE_DIM)
    lane = jnp.arange(LANES) % DIFF_DQK
    idx = jnp.where(lane < half, lane, jnp.where(lane < ROPE_DIM, lane - half, 0))
    invf = inv_freq[idx][None, :]
    pos = jnp.broadcast_to(positions.astype(F32)[:, None], (seq, LANES))
    table = jax.ShapeDtypeStruct((seq, LANES), F32)
    return pl.pallas_call(
        _rope_table_kernel,
        out_shape=(table, table, table),
        name="rope_tables",
    )(pos, invf)


def _inproj_kernel(x_ref, nw_ref, w_ref, wg_ref, c_ref, s1_ref, s2_ref, alog_ref, dtb_ref,
                   convw_ref, qkv_ref, z_ref, gate_ref, gate_t_ref, qk_ref, v_ref, pre_ref, *,
                   tiles_per_seq):
    hn = _rms(x_ref[...], nw_ref[...]).astype(BF16)
    tm = x_ref.shape[0]

    @pl.when((pl.program_id(0) % tiles_per_seq) == 0)
    def _():
        pre_ref[0:CONV_CARRY, :] = jnp.zeros((CONV_CARRY, DN_CONV), F32)

    pre_ref[CONV_CARRY:CONV_CARRY + tm, :] = _dot(hn, w_ref[:, 0:DN_CONV])
    for i in range(DN_CONV // LANES):
        cols = slice(i * LANES, (i + 1) * LANES)
        cw = convw_ref[:, cols]
        for r0 in range(0, tm, CONV_ROWS):
            win = pre_ref[r0:r0 + CONV_CARRY + CONV_ROWS, cols]
            y = cw[CONV_WIDTH - 1:CONV_WIDTH, :] * win[CONV_CARRY:, :]
            for shift in range(1, CONV_WIDTH):
                tap = CONV_WIDTH - 1 - shift
                y = y + cw[tap:tap + 1, :] * pltpu.roll(win, shift, 0)[CONV_CARRY:, :]
            y = _silu(y)
            if i < 2 * DN_HEADS:
                scale = DN_DK ** -0.5 if i < DN_HEADS else 1.0
                y = y * (lax.rsqrt(jnp.sum(y * y, axis=-1, keepdims=True) + 1e-6) * scale)
            qkv_ref[r0:r0 + CONV_ROWS, cols] = y.astype(BF16)
    pre_ref[0:CONV_CARRY, :] = pre_ref[tm:tm + CONV_CARRY, :]
    z_ref[...] = _dot(hn, w_ref[:, DN_CONV:DN_CONV + DN_V]).astype(BF16)
    off = DN_CONV + DN_V
    qk = _dot(hn, w_ref[:, off:off + 2 * DF_QK])
    c, s1, s2 = c_ref[...], s1_ref[...], s2_ref[...]
    half = ROPE_DIM // 2
    for i in range(2 * DF_QK // LANES):
        blk = qk[:, i * LANES:(i + 1) * LANES]
        rot = (blk * c + pltpu.roll(blk, half, 1) * s1 + pltpu.roll(blk, LANES - half, 1) * s2)
        qk_ref[:, i * LANES:(i + 1) * LANES] = rot.astype(BF16)
    off += 2 * DF_QK
    v_ref[...] = _dot(hn, w_ref[:, off:off + DF_V]).astype(BF16)
    gpre = _dot(hn, wg_ref[...])
    lane = lax.broadcasted_iota(jnp.int32, gpre.shape, 1)
    pos_in_chunk = lax.broadcasted_iota(jnp.int32, gpre.shape, 0) & (CHUNK - 1)
    beta = 1.0 / (1.0 + jnp.exp(-gpre))
    t = gpre + dtb_ref[...]
    softplus = jnp.maximum(t, 0.0) + jnp.log1p(jnp.exp(-jnp.abs(t)))
    gcum = -jnp.exp(alog_ref[...]) * softplus
    shift = 1
    while shift < CHUNK:
        gcum = gcum + jnp.where(pos_in_chunk >= shift, pltpu.roll(gcum, shift, 0), 0.0)
        shift *= 2
    gates = jnp.where(lane < DN_HEADS, beta, gcum)
    gate_ref[...] = gates
    gate_t_ref[...] = gates.T[0:N_GATES, :]


def _input_projection(x2d, seq, norm_w, w_main, w_gate, tables, alog_row, dtb_row, conv_w):
    tokens = x2d.shape[0]
    tm = min(TOKEN_TILE, seq)
    tiles_per_seq = seq // tm
    row = lambda i: (i, 0)
    pos = lambda i: (i % tiles_per_seq, 0)
    out_shape = (
        jax.ShapeDtypeStruct((tokens, DN_CONV), BF16),
        jax.ShapeDtypeStruct((tokens, DN_V), BF16),
        jax.ShapeDtypeStruct((tokens, LANES), F32),
        jax.ShapeDtypeStruct((N_GATES, tokens), F32),
        jax.ShapeDtypeStruct((tokens, 2 * DF_QK), BF16),
        jax.ShapeDtypeStruct((tokens, DF_V), BF16),
    )
    return pl.pallas_call(
        functools.partial(_inproj_kernel, tiles_per_seq=tiles_per_seq),
        grid=(tokens // tm,),
        in_specs=[
            pl.BlockSpec((tm, D_MODEL), row),
            _resident((1, D_MODEL)),
            _resident((D_MODEL, MAIN_N)),
            _resident((D_MODEL, LANES)),
            pl.BlockSpec((tm, LANES), pos),
            pl.BlockSpec((tm, LANES), pos),
            pl.BlockSpec((tm, LANES), pos),
            _resident((1, LANES)),
            _resident((1, LANES)),
            _resident((CONV_WIDTH, DN_CONV)),
        ],
        out_specs=(
            pl.BlockSpec((tm, DN_CONV), row),
            pl.BlockSpec((tm, DN_V), row),
            pl.BlockSpec((tm, LANES), row),
            pl.BlockSpec((N_GATES, tm), lambda i: (0, i)),
            pl.BlockSpec((tm, 2 * DF_QK), row),
            pl.BlockSpec((tm, DF_V), row),
        ),
        out_shape=out_shape,
        scratch_shapes=[pltpu.VMEM((CONV_CARRY + tm, DN_CONV), F32)],
        compiler_params=pltpu.CompilerParams(
            dimension_semantics=("arbitrary",), vmem_limit_bytes=VMEM_LIMIT),
        name="input_projection",
    )(x2d, norm_w, w_main, w_gate, *tables, alog_row, dtb_row, conv_w)


def _deltanet_kernel(qkv_ref, z_ref, gate_ref, gate_t_ref, nw_ref, gs_ref, out_ref,
                     pq_s, n_s, o_s, dec_s, state_ref, *, seq):
    n_chunks = seq // CHUNK
    C = CHUNK
    ii = lax.broadcasted_iota(jnp.int32, (C, C), 0)
    jj = lax.broadcasted_iota(jnp.int32, (C, C), 1)
    eye = (ii == jj).astype(F32)
    lane = lax.broadcasted_iota(jnp.int32, (C, LANES), 1)

    def precompute(chains):
        st = []
        for c, h in chains:
            r0 = c * C if isinstance(c, int) else pl.multiple_of(c * C, C)
            rows = pl.ds(r0, C)
            qn, kn, v = (qkv_ref[rows, part * DN_QK + h * LANES:part * DN_QK + (h + 1) * LANES]
                         .astype(F32) for part in range(3))
            gates = gate_ref[rows, :]
            beta = jnp.sum(jnp.where(lane == h, gates, 0.0), axis=-1, keepdims=True)
            g_col = jnp.sum(jnp.where(lane == DN_HEADS + h, gates, 0.0), axis=-1, keepdims=True)
            g_row = gate_t_ref[DN_HEADS + h, pl.ds(c, 1), :]
            dec_incl = jnp.exp(jnp.where(ii >= jj, g_col - g_row, -1e30))
            g_last = g_col[C - 1:C, :]
            eg = jnp.exp(g_col)
            kb = kn * beta
            st.append(dict(
                c=c, h=h, rows=rows, dec_incl=dec_incl, qs=qn * eg, kb=kb, knb=kn.astype(BF16),
                lhs=jnp.concatenate([kb, qn], axis=0).astype(BF16),
                rhs=jnp.concatenate([kb * eg, v * beta], axis=1).astype(BF16),
                kd=(kn * jnp.exp(g_last - g_col)).astype(BF16),
                dec=jnp.broadcast_to(jnp.exp(g_last), (8, LANES))))
        for d in st:
            kq = _dot_nt(d["lhs"], d["knb"])
            d["qk"] = (kq[C:2 * C, :] * d["dec_incl"]).astype(BF16)
            d["n_pow"] = -(kq[0:C, :] * jnp.where(ii > jj, d["dec_incl"], 0.0))
            d["t_inv"] = eye + d["n_pow"]
        for _ in range(int(math.log2(C)) - 1):
            for d in st:
                nb = d["n_pow"].astype(BF16)
                d["n_pow"] = _dot(nb, nb)
            for d in st:
                d["t_inv"] = d["t_inv"] + _dot(d["t_inv"].astype(BF16), d["n_pow"].astype(BF16))
        for d in st:
            d["wu"] = _dot(d["t_inv"].astype(BF16), d["rhs"]).astype(BF16)
        for d in st:
            d["pn"] = _dot_tn(d["kd"], d["wu"])
            d["qo"] = _dot(d["qk"], d["wu"])
        for d in st:
            c, h, pn, qo = d["c"], d["h"], d["pn"], d["qo"]
            pq_s[h, c, 0:DN_DK, :] = pn[:, 0:DN_DV].astype(BF16)
            pq_s[h, c, DN_DK:DN_DK + C, :] = (d["qs"] - qo[:, 0:DN_DV]).astype(BF16)
            n_s[h, c] = pn[:, DN_DV:2 * DN_DV].astype(BF16)
            o_s[h, d["rows"], :] = qo[:, DN_DV:2 * DN_DV]
            dec_s[h, c] = d["dec"]

    nw = nw_ref[...]

    def recur(group):
        states = [state_ref[h] for h in range(DN_HEADS)]
        for sub in range(DN_UNROLL):
            c = group * DN_UNROLL + sub
            r0 = c * C if isinstance(c, int) else pl.multiple_of(c * C, C)
            rows = pl.ds(r0, C)
            for h in range(DN_HEADS):
                s = states[h]
                ps = _dot(pq_s[h, c], s.astype(BF16))
                states[h] = s * dec_s[h, c][0:1, :] + n_s[h, c].astype(F32) - ps[0:DN_DK, :]
                o = ps[DN_DK:DN_DK + C, :] + o_s[h, rows, :]
                z = z_ref[rows, h * LANES:(h + 1) * LANES].astype(F32)
                y = _rms(o, nw) * _silu(z) * gs_ref[:, h * LANES:(h + 1) * LANES]
                out_ref[rows, h * LANES:(h + 1) * LANES] = y.astype(BF16)
        for h in range(DN_HEADS):
            state_ref[h] = states[h]

    def group_chains(group):
        return [(group * DN_UNROLL + sub, h) for sub in range(DN_UNROLL) for h in range(DN_HEADS)]

    n_groups = n_chunks // DN_UNROLL
    state_ref[...] = jnp.zeros(state_ref.shape, F32)
    precompute(group_chains(0))

    def step(t, carry):
        recur(t - 1)
        precompute(group_chains(t))
        return carry

    lax.fori_loop(1, n_groups, step, 0)
    recur(n_groups - 1)


def _deltanet(qkv, z, gates, gates_t, norm_w, gs):
    batch, seq, _ = qkv.shape
    n_chunks = seq // CHUNK
    assert n_chunks % DN_UNROLL == 0
    per_batch = lambda width: pl.BlockSpec((None, seq, width), lambda b: (b, 0, 0))
    return pl.pallas_call(
        functools.partial(_deltanet_kernel, seq=seq),
        grid=(batch,),
        in_specs=[
            per_batch(DN_CONV),
            per_batch(DN_V),
            per_batch(LANES),
            pl.BlockSpec((N_GATES, n_chunks, CHUNK), lambda b: (0, b, 0)),
            _resident((1, DN_DV)),
            _resident((1, DN_V)),
        ],
        out_specs=per_batch(DN_V),
        out_shape=jax.ShapeDtypeStruct((batch, seq, DN_V), BF16),
        scratch_shapes=[
            pltpu.VMEM((DN_HEADS, n_chunks, DN_DK + CHUNK, DN_DV), BF16),
            pltpu.VMEM((DN_HEADS, n_chunks, DN_DK, DN_DV), BF16),
            pltpu.VMEM((DN_HEADS, seq, DN_DV), F32),
            pltpu.VMEM((DN_HEADS, n_chunks, 8, LANES), F32),
            pltpu.VMEM((DN_HEADS, DN_DK, DN_DV), F32),
        ],
        compiler_params=pltpu.CompilerParams(
            dimension_semantics=("arbitrary",), vmem_limit_bytes=VMEM_LIMIT),
        name="gated_deltanet",
    )(qkv, z, gates, gates_t, norm_w, gs)


def _diff_attn_kernel(lam_ref, nw_ref, gs_ref, q_ref, k_ref, v_ref, o_ref, vext_ref, *, seq, block,
                      lambda_init):
    lp = lam_ref[...]
    lam = (jnp.exp(jnp.sum(lp[0:1, :] * lp[1:2, :], axis=-1, keepdims=True))
           - jnp.exp(jnp.sum(lp[2:3, :] * lp[3:4, :], axis=-1, keepdims=True)) + lambda_init)
    lane = lax.broadcasted_iota(jnp.int32, (block, LANES), 1)
    first_map = lane < DIFF_DQK
    row = lax.broadcasted_iota(jnp.int32, (block, block), 0)
    col = lax.broadcasted_iota(jnp.int32, (block, block), 1)
    causal = row >= col
    half = block // 2
    scale = DIFF_DQK ** -0.5
    zero = jnp.zeros((), BF16)

    vext_ref[:, 0:DIFF_DV] = v_ref[...]
    vext_ref[:, DIFF_DV:2 * DIFF_DV] = jnp.ones((seq, DIFF_DV), BF16)

    def update(carry, qs, kb, vext, mask):
        scores = [_dot_nt(qm, kb) for qm in qs]
        if mask is not None:
            scores = [jnp.where(mask, s, -1e30) for s in scores]
        m_new = [jnp.maximum(m, jnp.max(s, axis=-1, keepdims=True))
                 for s, (m, _) in zip(scores, carry)]
        probs = [jnp.exp((s - mn).astype(BF16)) for s, mn in zip(scores, m_new)]
        pv = [_dot(p, vext) for p in probs]
        return tuple((mn, jnp.exp(m - mn) * acc + x) for mn, (m, acc), x in zip(m_new, carry, pv))

    for qi in range(seq // block):
        q = q_ref[qi * block:(qi + 1) * block, :] * jnp.asarray(scale, BF16)
        qs = (jnp.where(first_map, q, zero), jnp.where(first_map, zero, q))
        init = (jnp.full((block, 1), -1e30, F32), jnp.zeros((block, 2 * DIFF_DV), F32))

        carry = (init, init)
        for j in range(qi):
            keys = slice(j * block, (j + 1) * block)
            carry = update(carry, qs, k_ref[keys, :], vext_ref[keys, :], None)
        rows = slice(qi * block, (qi + 1) * block)
        keys_a = slice(qi * block, qi * block + half)
        keys_b = slice(qi * block + half, (qi + 1) * block)
        carry = update(carry, qs, k_ref[keys_a, :], vext_ref[keys_a, :], causal[:, 0:half])
        bottom = update(tuple((m[half:, :], acc[half:, :]) for m, acc in carry),
                        tuple(qm[half:, :] for qm in qs), k_ref[keys_b, :], vext_ref[keys_b, :],
                        causal[0:half, 0:half])
        acc1, acc2 = (jnp.concatenate([top[1][0:half, :], bot[1]], axis=0)
                      for top, bot in zip(carry, bottom))
        a1, l1 = acc1[:, 0:DIFF_DV], acc1[:, DIFF_DV:2 * DIFF_DV]
        a2, l2 = acc2[:, 0:DIFF_DV], acc2[:, DIFF_DV:2 * DIFF_DV]
        o = a1 / l1 - lam * (a2 / l2)
        y = _rms(o, nw_ref[...]) * (1.0 - lambda_init) * gs_ref[...]
        o_ref[rows, :] = y.astype(BF16)


def _diff_attention(qk, v, lam_params, norm_w, gs, lambda_init):
    batch, seq, _ = v.shape
    block = min(ATTN_BLOCK, seq)
    head_cols = lambda off: pl.BlockSpec((None, seq, LANES), lambda b, h: (b, 0, off + h))
    return pl.pallas_call(
        functools.partial(_diff_attn_kernel, seq=seq, block=block, lambda_init=lambda_init),
        grid=(batch, DIFF_HEADS),
        in_specs=[
            _resident((4, DIFF_DQK)),
            _resident((1, DIFF_DV)),
            pl.BlockSpec((1, LANES), lambda b, h: (0, h)),
            head_cols(0),
            head_cols(DIFF_HEADS),
            head_cols(0),
        ],
        out_specs=head_cols(0),
        out_shape=jax.ShapeDtypeStruct((batch, seq, DF_V), BF16),
        scratch_shapes=[pltpu.VMEM((seq, 2 * DIFF_DV), BF16)],
        compiler_params=pltpu.CompilerParams(
            dimension_semantics=("arbitrary", "arbitrary"), vmem_limit_bytes=VMEM_LIMIT),
        name="diff_attention",
    )(lam_params, norm_w, gs, qk, qk, v)


def _out_mlp_kernel(x_ref, odn_ref, odf_ref, wo_ref, nw_ref, wu_ref, wd_ref, fw_ref, y_ref,
                    hidden_ref, *, final_norm):
    x1 = (x_ref[...] + _dot(odn_ref[...], wo_ref[0:DN_V, :])
          + _dot(odf_ref[...], wo_ref[DN_V:DN_V + DF_V, :]))
    hm = _rms(x1, nw_ref[...]).astype(BF16)
    for c in range(D_FF // FF_CHUNK):
        cols = slice(c * FF_CHUNK, (c + 1) * FF_CHUNK)
        hidden_ref[:, cols] = jnp.square(jnp.maximum(_dot(hm, wu_ref[:, cols]), 0.0)).astype(BF16)
    y = x1 + _dot(hidden_ref[...], wd_ref[...])
    if final_norm:
        y = _rms(y, fw_ref[...])
    y_ref[...] = y


def _out_mlp(x2d, o_dn, o_df, w_out, norm_w, w_up, w_down, final_w, final_norm):
    tokens = x2d.shape[0]
    tm = min(TOKEN_TILE, tokens)
    row = lambda i: (i, 0)
    return pl.pallas_call(
        functools.partial(_out_mlp_kernel, final_norm=final_norm),
        grid=(tokens // tm,),
        in_specs=[
            pl.BlockSpec((tm, D_MODEL), row),
            pl.BlockSpec((tm, DN_V), row),
            pl.BlockSpec((tm, DF_V), row),
            _resident((DN_V + DF_V, D_MODEL)),
            _resident((1, D_MODEL)),
            _resident((D_MODEL, D_FF)),
            _resident((D_FF, D_MODEL)),
            _resident((1, D_MODEL)),
        ],
        out_specs=pl.BlockSpec((tm, D_MODEL), row),
        out_shape=jax.ShapeDtypeStruct((tokens, D_MODEL), F32),
        scratch_shapes=[pltpu.VMEM((tm, D_FF), BF16)],
        compiler_params=pltpu.CompilerParams(
            dimension_semantics=("arbitrary",), vmem_limit_bytes=VMEM_LIMIT),
        name="out_projection_mlp",
    )(x2d, o_dn, o_df, w_out, norm_w, w_up, w_down, final_w)


def _pad_lanes(vec, offset):
    return jnp.zeros((1, LANES), F32).at[0, offset:offset + vec.shape[0]].set(vec.astype(F32))


def kernel(x, positions, attn_norm_w, w_in, conv_w, a_log, dt_bias, dn_norm_w, lambda_q1, lambda_k1, lambda_q2, lambda_k2, diff_norm_w, group_scale, w_out, mlp_norm_w, w_up, w_down, final_norm_w):
    batch, seq, _ = x.shape
    depth = w_in.shape[0]
    tables = _rope_tables(positions)
    x2d = x.reshape(batch * seq, D_MODEL)
    for l in range(depth):
        wl = w_in[l]
        w_main = jnp.concatenate([wl[:, :GATE_OFF], wl[:, GATE_OFF + N_GATES:]], axis=1).astype(BF16)
        w_gate = jnp.pad(wl[:, GATE_OFF:GATE_OFF + N_GATES], ((0, 0), (0, LANES - N_GATES))).astype(BF16)
        qkv, z, gates, gates_t, qk, v = _input_projection(
            x2d, seq, attn_norm_w[l][None, :], w_main, w_gate, tables,
            _pad_lanes(a_log[l], DN_HEADS), _pad_lanes(dt_bias[l], DN_HEADS), conv_w[l].astype(F32))
        gs = group_scale[l].astype(F32)[None, :]
        o_dn = _deltanet(qkv.reshape(batch, seq, DN_CONV), z.reshape(batch, seq, DN_V),
                         gates.reshape(batch, seq, LANES),
                         gates_t.reshape(N_GATES, batch * seq // CHUNK, CHUNK),
                         dn_norm_w[l].astype(F32)[None, :], gs[:, :DN_V])
        lam_params = jnp.stack([lambda_q1[l], lambda_k1[l], lambda_q2[l], lambda_k2[l]]).astype(F32)
        lambda_init = 0.8 - 0.6 * math.exp(-0.3 * l)
        o_df = _diff_attention(qk.reshape(batch, seq, 2 * DF_QK), v.reshape(batch, seq, DF_V),
                               lam_params, diff_norm_w[l].astype(F32)[None, :], gs[:, DN_V:],
                               lambda_init)
        x2d = _out_mlp(x2d, o_dn.reshape(batch * seq, DN_V), o_df.reshape(batch * seq, DF_V),
                       w_out[l].astype(BF16), mlp_norm_w[l][None, :], w_up[l].astype(BF16),
                       w_down[l].astype(BF16), final_norm_w[None, :], l == depth - 1)
    return x2d.reshape(batch, seq, D_MODEL)
```

```python
import functools
import math

import jax
import jax.numpy as jnp
from jax import lax
from jax.experimental import pallas as pl
from jax.experimental.pallas import tpu as pltpu

F32 = jnp.float32
BF16 = jnp.bfloat16

D_MODEL = 1024
DN_HEADS = 4
DN_DK = 128
DN_DV = 128
CONV_WIDTH = 4
CHUNK = 64
DIFF_HEADS = 4
DIFF_DQK = 64
DIFF_DV = 128
ROPE_THETA = 500000.0
ROPE_DIM = 16
D_FF = 4 * D_MODEL
EPS = 1e-6

DN_QK = DN_HEADS * DN_DK
DN_V = DN_HEADS * DN_DV
DN_CONV = 2 * DN_QK + DN_V
DF_QK = DIFF_HEADS * 2 * DIFF_DQK
DF_V = DIFF_HEADS * DIFF_DV
GATE_OFF = DN_CONV + DN_V
N_GATES = 2 * DN_HEADS
MAIN_N = DN_CONV + DN_V + 2 * DF_QK + DF_V
LANES = 128
CONV_CARRY = 8
CONV_ROWS = 64

VMEM_LIMIT = 56 * 1024 * 1024
TOKEN_TILE = 512
ATTN_BLOCK = 512
FF_CHUNK = 1024
DN_UNROLL = 8


def _dot(a, b):
    return jnp.dot(a, b, preferred_element_type=F32)


def _dot_nt(a, b):
    return lax.dot_general(a, b, (((1,), (1,)), ((), ())), preferred_element_type=F32)


def _dot_tn(a, b):
    return lax.dot_general(a, b, (((0,), (0,)), ((), ())), preferred_element_type=F32)


def _rms(x, w):
    return x * lax.rsqrt(jnp.mean(x * x, axis=-1, keepdims=True) + EPS) * w


def _silu(x):
    return x / (1.0 + jnp.exp(-x))


def _resident(shape):
    zeros = (0,) * len(shape)
    return pl.BlockSpec(shape, lambda *_: zeros, pipeline_mode=pl.Buffered(1))


def _rope_table_kernel(pos_ref, invf_ref, c_ref, s1_ref, s2_ref):
    ang = pos_ref[...] * invf_ref[...]
    lane = lax.broadcasted_iota(jnp.int32, ang.shape, 1) & (DIFF_DQK - 1)
    c = jnp.cos(ang)
    s = jnp.sin(ang)
    half = ROPE_DIM // 2
    c_ref[...] = jnp.where(lane < ROPE_DIM, c, 1.0)
    s1_ref[...] = jnp.where((lane >= half) & (lane < ROPE_DIM), s, 0.0)
    s2_ref[...] = jnp.where(lane < half, -s, 0.0)


def _rope_tables(positions):
    seq = positions.shape[0]
    half = ROPE_DIM // 2
    inv_freq = ROPE_THETA ** (-jnp.arange(0, ROPE_DIM, 2, dtype=F32) / ROPE_DIM)
    lane = jnp.arange(LANES) % DIFF_DQK
    idx = jnp.where(lane < half, lane, jnp.where(lane < ROPE_DIM, lane - half, 0))
    invf = inv_freq[idx][None, :]
    pos = jnp.broadcast_to(positions.astype(F32)[:, None], (seq, LANES))
    table = jax.ShapeDtypeStruct((seq, LANES), F32)
    return pl.pallas_call(
        _rope_table_kernel,
        out_shape=(table, table, table),
        name="rope_tables",
    )(pos, invf)


def _inproj_kernel(x_ref, nw_ref, w_ref, wg_ref, c_ref, s1_ref, s2_ref, alog_ref, dtb_ref,
                   convw_ref, qkv_ref, z_ref, gate_ref, gate_t_ref, qk_ref, v_ref, pre_ref, *,
                   tiles_per_seq):
    hn = _rms(x_ref[...], nw_ref[...]).astype(BF16)
    tm = x_ref.shape[0]

    @pl.when((pl.program_id(0) % tiles_per_seq) == 0)
    def _():
        pre_ref[0:CONV_CARRY, :] = jnp.zeros((CONV_CARRY, DN_CONV), F32)

    pre_ref[CONV_CARRY:CONV_CARRY + tm, :] = _dot(hn, w_ref[:, 0:DN_CONV])
    for i in range(DN_CONV // LANES):
        cols = slice(i * LANES, (i + 1) * LANES)
        cw = convw_ref[:, cols]
        for r0 in range(0, tm, CONV_ROWS):
            win = pre_ref[r0:r0 + CONV_CARRY + CONV_ROWS, cols]
            y = cw[CONV_WIDTH - 1:CONV_WIDTH, :] * win[CONV_CARRY:, :]
            for shift in range(1, CONV_WIDTH):
                tap = CONV_WIDTH - 1 - shift
                y = y + cw[tap:tap + 1, :] * pltpu.roll(win, shift, 0)[CONV_CARRY:, :]
            y = _silu(y)
            if i < 2 * DN_HEADS:
                scale = DN_DK ** -0.5 if i < DN_HEADS else 1.0
                y = y * (lax.rsqrt(jnp.sum(y * y, axis=-1, keepdims=True) + 1e-6) * scale)
            qkv_ref[r0:r0 + CONV_ROWS, cols] = y.astype(BF16)
    pre_ref[0:CONV_CARRY, :] = pre_ref[tm:tm + CONV_CARRY, :]
    z_ref[...] = _dot(hn, w_ref[:, DN_CONV:DN_CONV + DN_V]).astype(BF16)
    off = DN_CONV + DN_V
    qk = _dot(hn, w_ref[:, off:off + 2 * DF_QK])
    c, s1, s2 = c_ref[...], s1_ref[...], s2_ref[...]
    half = ROPE_DIM // 2
    for i in range(2 * DF_QK // LANES):
        blk = qk[:, i * LANES:(i + 1) * LANES]
        rot = (blk * c + pltpu.roll(blk, half, 1) * s1 + pltpu.roll(blk, LANES - half, 1) * s2)
        qk_ref[:, i * LANES:(i + 1) * LANES] = rot.astype(BF16)
    off += 2 * DF_QK
    v_ref[...] = _dot(hn, w_ref[:, off:off + DF_V]).astype(BF16)
    gpre = _dot(hn, wg_ref[...])
    lane = lax.broadcasted_iota(jnp.int32, gpre.shape, 1)
    pos_in_chunk = lax.broadcasted_iota(jnp.int32, gpre.shape, 0) & (CHUNK - 1)
    beta = 1.0 / (1.0 + jnp.exp(-gpre))
    t = gpre + dtb_ref[...]
    softplus = jnp.maximum(t, 0.0) + jnp.log1p(jnp.exp(-jnp.abs(t)))
    gcum = -jnp.exp(alog_ref[...]) * softplus
    shift = 1
    while shift < CHUNK:
        gcum = gcum + jnp.where(pos_in_chunk >= shift, pltpu.roll(gcum, shift, 0), 0.0)
        shift *= 2
    gates = jnp.where(lane < DN_HEADS, beta, gcum)
    gate_ref[...] = gates
    gate_t_ref[...] = gates.T[0:N_GATES, :]


def _input_projection(x2d, seq, norm_w, w_main, w_gate, tables, alog_row, dtb_row, conv_w):
    tokens = x2d.shape[0]
    tm = min(TOKEN_TILE, seq)
    tiles_per_seq = seq // tm
    row = lambda i: (i, 0)
    pos = lambda i: (i % tiles_per_seq, 0)
    out_shape = (
        jax.ShapeDtypeStruct((tokens, DN_CONV), BF16),
        jax.ShapeDtypeStruct((tokens, DN_V), BF16),
        jax.ShapeDtypeStruct((tokens, LANES), F32),
        jax.ShapeDtypeStruct((N_GATES, tokens), F32),
        jax.ShapeDtypeStruct((tokens, 2 * DF_QK), BF16),
        jax.ShapeDtypeStruct((tokens, DF_V), BF16),
    )
    return pl.pallas_call(
        functools.partial(_inproj_kernel, tiles_per_seq=tiles_per_seq),
        grid=(tokens // tm,),
        in_specs=[
            pl.BlockSpec((tm, D_MODEL), row),
            _resident((1, D_MODEL)),
            _resident((D_MODEL, MAIN_N)),
            _resident((D_MODEL, LANES)),
            pl.BlockSpec((tm, LANES), pos),
            pl.BlockSpec((tm, LANES), pos),
            pl.BlockSpec((tm, LANES), pos),
            _resident((1, LANES)),
            _resident((1, LANES)),
            _resident((CONV_WIDTH, DN_CONV)),
        ],
        out_specs=(
            pl.BlockSpec((tm, DN_CONV), row),
            pl.BlockSpec((tm, DN_V), row),
            pl.BlockSpec((tm, LANES), row),
            pl.BlockSpec((N_GATES, tm), lambda i: (0, i)),
            pl.BlockSpec((tm, 2 * DF_QK), row),
            pl.BlockSpec((tm, DF_V), row),
        ),
        out_shape=out_shape,
        scratch_shapes=[pltpu.VMEM((CONV_CARRY + tm, DN_CONV), F32)],
        compiler_params=pltpu.CompilerParams(
            dimension_semantics=("arbitrary",), vmem_limit_bytes=VMEM_LIMIT),
        name="input_projection",
    )(x2d, norm_w, w_main, w_gate, *tables, alog_row, dtb_row, conv_w)


def _deltanet_kernel(qkv_ref, z_ref, gate_ref, gate_t_ref, nw_ref, gs_ref, out_ref,
                     pq_s, n_s, o_s, dec_s, state_ref, *, seq):
    n_chunks = seq // CHUNK
    C = CHUNK
    ii = lax.broadcasted_iota(jnp.int32, (C, C), 0)
    jj = lax.broadcasted_iota(jnp.int32, (C, C), 1)
    eye = (ii == jj).astype(F32)
    lane = lax.broadcasted_iota(jnp.int32, (C, LANES), 1)

    def precompute(chains):
        st = []
        for c, h in chains:
            r0 = c * C if isinstance(c, int) else pl.multiple_of(c * C, C)
            rows = pl.ds(r0, C)
            qn, kn, v = (qkv_ref[rows, part * DN_QK + h * LANES:part * DN_QK + (h + 1) * LANES]
                         .astype(F32) for part in range(3))
            gates = gate_ref[rows, :]
            beta = jnp.sum(jnp.where(lane == h, gates, 0.0), axis=-1, keepdims=True)
            g_col = jnp.sum(jnp.where(lane == DN_HEADS + h, gates, 0.0), axis=-1, keepdims=True)
            g_row = gate_t_ref[DN_HEADS + h, pl.ds(c, 1), :]
            dec_incl = jnp.exp(jnp.where(ii >= jj, g_col - g_row, -1e30))
            g_last = g_col[C - 1:C, :]
            eg = jnp.exp(g_col)
            kb = kn * beta
            st.append(dict(
                c=c, h=h, rows=rows, dec_incl=dec_incl, qs=qn * eg, kb=kb, knb=kn.astype(BF16),
                lhs=jnp.concatenate([kb, qn], axis=0).astype(BF16),
                rhs=jnp.concatenate([kb * eg, v * beta], axis=1).astype(BF16),
                kd=(kn * jnp.exp(g_last - g_col)).astype(BF16),
                dec=jnp.broadcast_to(jnp.exp(g_last), (8, LANES))))
        for d in st:
            kq = _dot_nt(d["lhs"], d["knb"])
            d["qk"] = (kq[C:2 * C, :] * d["dec_incl"]).astype(BF16)
            d["n_pow"] = -(kq[0:C, :] * jnp.where(ii > jj, d["dec_incl"], 0.0))
            d["t_inv"] = eye + d["n_pow"]
        for _ in range(int(math.log2(C)) - 1):
            for d in st:
                nb = d["n_pow"].astype(BF16)
                d["n_pow"] = _dot(nb, nb)
            for d in st:
                d["t_inv"] = d["t_inv"] + _dot(d["t_inv"].astype(BF16), d["n_pow"].astype(BF16))
        for d in st:
            d["wu"] = _dot(d["t_inv"].astype(BF16), d["rhs"]).astype(BF16)
        for d in st:
            d["pn"] = _dot_tn(d["kd"], d["wu"])
            d["qo"] = _dot(d["qk"], d["wu"])
        for d in st:
            c, h, pn, qo = d["c"], d["h"], d["pn"], d["qo"]
            pq_s[h, c, 0:DN_DK, :] = pn[:, 0:DN_DV].astype(BF16)
            pq_s[h, c, DN_DK:DN_DK + C, :] = (d["qs"] - qo[:, 0:DN_DV]).astype(BF16)
            n_s[h, c] = pn[:, DN_DV:2 * DN_DV].astype(BF16)
            o_s[h, d["rows"], :] = qo[:, DN_DV:2 * DN_DV]
            dec_s[h, c] = d["dec"]

    nw = nw_ref[...]

    def recur(group):
        states = [state_ref[h] for h in range(DN_HEADS)]
        for sub in range(DN_UNROLL):
            c = group * DN_UNROLL + sub
            r0 = c * C if isinstance(c, int) else pl.multiple_of(c * C, C)
            rows = pl.ds(r0, C)
            for h in range(DN_HEADS):
                s = states[h]
                ps = _dot(pq_s[h, c], s.astype(BF16))
                states[h] = s * dec_s[h, c][0:1, :] + n_s[h, c].astype(F32) - ps[0:DN_DK, :]
                o = ps[DN_DK:DN_DK + C, :] + o_s[h, rows, :]
                z = z_ref[rows, h * LANES:(h + 1) * LANES].astype(F32)
                y = _rms(o, nw) * _silu(z) * gs_ref[:, h * LANES:(h + 1) * LANES]
                out_ref[rows, h * LANES:(h + 1) * LANES] = y.astype(BF16)
        for h in range(DN_HEADS):
            state_ref[h] = states[h]

    def group_chains(group):
        return [(group * DN_UNROLL + sub, h) for sub in range(DN_UNROLL) for h in range(DN_HEADS)]

    n_groups = n_chunks // DN_UNROLL
    state_ref[...] = jnp.zeros(state_ref.shape, F32)
    precompute(group_chains(0))

    def step(t, carry):
        recur(t - 1)
        precompute(group_chains(t))
        return carry

    lax.fori_loop(1, n_groups, step, 0)
    recur(n_groups - 1)


def _deltanet(qkv, z, gates, gates_t, norm_w, gs):
    batch, seq, _ = qkv.shape
    n_chunks = seq // CHUNK
    assert n_chunks % DN_UNROLL == 0
    per_batch = lambda width: pl.BlockSpec((None, seq, width), lambda b: (b, 0, 0))
    return pl.pallas_call(
        functools.partial(_deltanet_kernel, seq=seq),
        grid=(batch,),
        in_specs=[
            per_batch(DN_CONV),
            per_batch(DN_V),
            per_batch(LANES),
            pl.BlockSpec((N_GATES, n_chunks, CHUNK), lambda b: (0, b, 0)),
            _resident((1, DN_DV)),
            _resident((1, DN_V)),
        ],
        out_specs=per_batch(DN_V),
        out_shape=jax.ShapeDtypeStruct((batch, seq, DN_V), BF16),
        scratch_shapes=[
            pltpu.VMEM((DN_HEADS, n_chunks, DN_DK + CHUNK, DN_DV), BF16),
            pltpu.VMEM((DN_HEADS, n_chunks, DN_DK, DN_DV), BF16),
            pltpu.VMEM((DN_HEADS, seq, DN_DV), F32),
            pltpu.VMEM((DN_HEADS, n_chunks, 8, LANES), F32),
            pltpu.VMEM((DN_HEADS, DN_DK, DN_DV), F32),
        ],
        compiler_params=pltpu.CompilerParams(
            dimension_semantics=("arbitrary",), vmem_limit_bytes=VMEM_LIMIT),
        name="gated_deltanet",
    )(qkv, z, gates, gates_t, norm_w, gs)


def _diff_attn_kernel(lam_ref, nw_ref, gs_ref, q_ref, k_ref, v_ref, o_ref, vext_ref, *, seq, block,
                      lambda_init):
    lp = lam_ref[...]
    lam = (jnp.exp(jnp.sum(lp[0:1, :] * lp[1:2, :], axis=-1, keepdims=True))
           - jnp.exp(jnp.sum(lp[2:3, :] * lp[3:4, :], axis=-1, keepdims=True)) + lambda_init)
    lane = lax.broadcasted_iota(jnp.int32, (block, LANES), 1)
    first_map = lane < DIFF_DQK
    row = lax.broadcasted_iota(jnp.int32, (block, block), 0)
    col = lax.broadcasted_iota(jnp.int32, (block, block), 1)
    causal = row >= col
    half = block // 2
    scale = DIFF_DQK ** -0.5
    zero = jnp.zeros((), BF16)

    vext_ref[:, 0:DIFF_DV] = v_ref[...]
    vext_ref[:, DIFF_DV:2 * DIFF_DV] = jnp.ones((seq, DIFF_DV), BF16)

    def scores(qs, keys, mask):
        kb = k_ref[keys, :]
        out = [_dot_nt(qm, kb) for qm in qs]
        if mask is not None:
            out = [jnp.where(mask, s, -1e30) for s in out]
        return out

    def absorb(carry, ss, keys):
        vext = vext_ref[keys, :]
        m_new = [jnp.maximum(m, jnp.max(s, axis=-1, keepdims=True)) for s, (m, _) in zip(ss, carry)]
        probs = [jnp.exp((s - mn).astype(BF16)) for s, mn in zip(ss, m_new)]
        pv = [_dot(p, vext) for p in probs]
        return tuple((mn, jnp.exp(m - mn) * acc + x) for mn, (m, acc), x in zip(m_new, carry, pv))

    def finish(qi, top, bottom):
        acc1, acc2 = (jnp.concatenate([t[1][0:half, :], b[1]], axis=0) for t, b in zip(top, bottom))
        a1, l1 = acc1[:, 0:DIFF_DV], acc1[:, DIFF_DV:2 * DIFF_DV]
        a2, l2 = acc2[:, 0:DIFF_DV], acc2[:, DIFF_DV:2 * DIFF_DV]
        o = a1 / l1 - lam * (a2 / l2)
        y = _rms(o, nw_ref[...]) * (1.0 - lambda_init) * gs_ref[...]
        o_ref[qi * block:(qi + 1) * block, :] = y.astype(BF16)

    n_blocks = seq // block
    per_block = []
    for qi in range(n_blocks):
        todo = [(qi, "all", slice(j * block, (j + 1) * block), None) for j in range(qi)]
        todo.append((qi, "all", slice(qi * block, qi * block + half), causal[:, 0:half]))
        todo.append((qi, "bottom", slice(qi * block + half, (qi + 1) * block),
                     causal[0:half, 0:half]))
        per_block.append(todo)
    items = []
    while any(per_block):
        for todo in reversed(per_block):
            if todo:
                items.append(todo.pop(0))

    queries = []
    for qi in range(n_blocks):
        q = q_ref[qi * block:(qi + 1) * block, :] * jnp.asarray(scale, BF16)
        queries.append((jnp.where(first_map, q, zero), jnp.where(first_map, zero, q)))
    init = (jnp.full((block, 1), -1e30, F32), jnp.zeros((block, 2 * DIFF_DV), F32))
    carries = [(init, init) for _ in range(n_blocks)]

    def item_scores(item):
        qi, part, keys, mask = item
        qs = queries[qi] if part == "all" else tuple(qm[half:, :] for qm in queries[qi])
        return scores(qs, keys, mask)

    ss = item_scores(items[0])
    for k, (qi, part, keys, _) in enumerate(items):
        ss_next = item_scores(items[k + 1]) if k + 1 < len(items) else None
        if part == "all":
            carries[qi] = absorb(carries[qi], ss, keys)
        else:
            bottom = absorb(tuple((m[half:, :], acc[half:, :]) for m, acc in carries[qi]), ss, keys)
            finish(qi, carries[qi], bottom)
        ss = ss_next


def _diff_attention(qk, v, lam_params, norm_w, gs, lambda_init):
    batch, seq, _ = v.shape
    block = min(ATTN_BLOCK, seq)
    head_cols = lambda off: pl.BlockSpec((None, seq, LANES), lambda b, h: (b, 0, off + h))
    return pl.pallas_call(
        functools.partial(_diff_attn_kernel, seq=seq, block=block, lambda_init=lambda_init),
        grid=(batch, DIFF_HEADS),
        in_specs=[
            _resident((4, DIFF_DQK)),
            _resident((1, DIFF_DV)),
            pl.BlockSpec((1, LANES), lambda b, h: (0, h)),
            head_cols(0),
            head_cols(DIFF_HEADS),
            head_cols(0),
        ],
        out_specs=head_cols(0),
        out_shape=jax.ShapeDtypeStruct((batch, seq, DF_V), BF16),
        scratch_shapes=[pltpu.VMEM((seq, 2 * DIFF_DV), BF16)],
        compiler_params=pltpu.CompilerParams(
            dimension_semantics=("arbitrary", "arbitrary"), vmem_limit_bytes=VMEM_LIMIT),
        name="diff_attention",
    )(lam_params, norm_w, gs, qk, qk, v)


def _out_mlp_kernel(x_ref, odn_ref, odf_ref, wo_ref, nw_ref, wu_ref, wd_ref, fw_ref, y_ref,
                    hidden_ref, *, final_norm):
    x1 = (x_ref[...] + _dot(odn_ref[...], wo_ref[0:DN_V, :])
          + _dot(odf_ref[...], wo_ref[DN_V:DN_V + DF_V, :]))
    hm = _rms(x1, nw_ref[...]).astype(BF16)
    for c in range(D_FF // FF_CHUNK):
        cols = slice(c * FF_CHUNK, (c + 1) * FF_CHUNK)
        hidden_ref[:, cols] = jnp.square(jnp.maximum(_dot(hm, wu_ref[:, cols]), 0.0)).astype(BF16)
    y = x1 + _dot(hidden_ref[...], wd_ref[...])
    if final_norm:
        y = _rms(y, fw_ref[...])
    y_ref[...] = y


def _out_mlp(x2d, o_dn, o_df, w_out, norm_w, w_up, w_down, final_w, final_norm):
    tokens = x2d.shape[0]
    tm = min(TOKEN_TILE, tokens)
    row = lambda i: (i, 0)
    return pl.pallas_call(
        functools.partial(_out_mlp_kernel, final_norm=final_norm),
        grid=(tokens // tm,),
        in_specs=[
            pl.BlockSpec((tm, D_MODEL), row),
            pl.BlockSpec((tm, DN_V), row),
            pl.BlockSpec((tm, DF_V), row),
            _resident((DN_V + DF_V, D_MODEL)),
            _resident((1, D_MODEL)),
            _resident((D_MODEL, D_FF)),
            _resident((D_FF, D_MODEL)),
            _resident((1, D_MODEL)),
        ],
        out_specs=pl.BlockSpec((tm, D_MODEL), row),
        out_shape=jax.ShapeDtypeStruct((tokens, D_MODEL), F32),
        scratch_shapes=[pltpu.VMEM((tm, D_FF), BF16)],
        compiler_params=pltpu.CompilerParams(
            dimension_semantics=("arbitrary",), vmem_limit_bytes=VMEM_LIMIT),
        name="out_projection_mlp",
    )(x2d, o_dn, o_df, w_out, norm_w, w_up, w_down, final_w)


def _pad_lanes(vec, offset):
    return jnp.zeros((1, LANES), F32).at[0, offset:offset + vec.shape[0]].set(vec.astype(F32))


def kernel(x, positions, attn_norm_w, w_in, conv_w, a_log, dt_bias, dn_norm_w, lambda_q1, lambda_k1, lambda_q2, lambda_k2, diff_norm_w, group_scale, w_out, mlp_norm_w, w_up, w_down, final_norm_w):
    batch, seq, _ = x.shape
    depth = w_in.shape[0]
    tables = _rope_tables(positions)
    x2d = x.reshape(batch * seq, D_MODEL)
    for l in range(depth):
        wl = w_in[l]
        w_main = jnp.concatenate([wl[:, :GATE_OFF], wl[:, GATE_OFF + N_GATES:]], axis=1).astype(BF16)
        w_gate = jnp.pad(wl[:, GATE_OFF:GATE_OFF + N_GATES], ((0, 0), (0, LANES - N_GATES))).astype(BF16)
        qkv, z, gates, gates_t, qk, v = _input_projection(
            x2d, seq, attn_norm_w[l][None, :], w_main, w_gate, tables,
            _pad_lanes(a_log[l], DN_HEADS), _pad_lanes(dt_bias[l], DN_HEADS), conv_w[l].astype(F32))
        gs = group_scale[l].astype(F32)[None, :]
        o_dn = _deltanet(qkv.reshape(batch, seq, DN_CONV), z.reshape(batch, seq, DN_V),
                         gates.reshape(batch, seq, LANES),
                         gates_t.reshape(N_GATES, batch * seq // CHUNK, CHUNK),
                         dn_norm_w[l].astype(F32)[None, :], gs[:, :DN_V])
        lam_params = jnp.stack([lambda_q1[l], lambda_k1[l], lambda_q2[l], lambda_k2[l]]).astype(F32)
        lambda_init = 0.8 - 0.6 * math.exp(-0.3 * l)
        o_df = _diff_attention(qk.reshape(batch, seq, 2 * DF_QK), v.reshape(batch, seq, DF_V),
                               lam_params, diff_norm_w[l].astype(F32)[None, :], gs[:, DN_V:],
                               lambda_init)
        x2d = _out_mlp(x2d, o_dn.reshape(batch * seq, DN_V), o_df.reshape(batch * seq, DF_V),
                       w_out[l].astype(BF16), mlp_norm_w[l][None, :], w_up[l].astype(BF16),
                       w_down[l].astype(BF16), final_norm_w[None, :], l == depth - 1)
    return x2d.reshape(batch, seq, D_MODEL)
```

```python
import functools
import math

import jax
import jax.numpy as jnp
from jax import lax
from jax.experimental import pallas as pl
from jax.experimental.pallas import tpu as pltpu

F32 = jnp.float32
BF16 = jnp.bfloat16

D_MODEL = 1024
DN_HEADS = 4
DN_DK = 128
DN_DV = 128
CONV_WIDTH = 4
CHUNK = 64
DIFF_HEADS = 4
DIFF_DQK = 64
DIFF_DV = 128
ROPE_THETA = 500000.0
ROPE_DIM = 16
D_FF = 4 * D_MODEL
EPS = 1e-6

DN_QK = DN_HEADS * DN_DK
DN_V = DN_HEADS * DN_DV
DN_CONV = 2 * DN_QK + DN_V
DF_QK = DIFF_HEADS * 2 * DIFF_DQK
DF_V = DIFF_HEADS * DIFF_DV
GATE_OFF = DN_CONV + DN_V
N_GATES = 2 * DN_HEADS
MAIN_N = DN_CONV + DN_V + 2 * DF_QK + DF_V
LANES = 128
CONV_CARRY = 8
CONV_ROWS = 64

VMEM_LIMIT = 56 * 1024 * 1024
TOKEN_TILE = 512
ATTN_BLOCK = 512
FF_CHUNK = 1024
DN_UNROLL = 8


def _dot(a, b):
    return jnp.dot(a, b, preferred_element_type=F32)


def _dot_nt(a, b):
    return lax.dot_general(a, b, (((1,), (1,)), ((), ())), preferred_element_type=F32)


def _dot_tn(a, b):
    return lax.dot_general(a, b, (((0,), (0,)), ((), ())), preferred_element_type=F32)


def _rms(x, w):
    return x * lax.rsqrt(jnp.mean(x * x, axis=-1, keepdims=True) + EPS) * w


def _silu(x):
    h = 0.5 * x
    return h + h * jnp.tanh(h)


def _resident(shape):
    zeros = (0,) * len(shape)
    return pl.BlockSpec(shape, lambda *_: zeros, pipeline_mode=pl.Buffered(1))


def _rope_table_kernel(pos_ref, invf_ref, c_ref, s1_ref, s2_ref):
    ang = pos_ref[...] * invf_ref[...]
    lane = lax.broadcasted_iota(jnp.int32, ang.shape, 1) & (DIFF_DQK - 1)
    c = jnp.cos(ang)
    s = jnp.sin(ang)
    half = ROPE_DIM // 2
    c_ref[...] = jnp.where(lane < ROPE_DIM, c, 1.0)
    s1_ref[...] = jnp.where((lane >= half) & (lane < ROPE_DIM), s, 0.0)
    s2_ref[...] = jnp.where(lane < half, -s, 0.0)


def _rope_tables(positions):
    seq = positions.shape[0]
    half = ROPE_DIM // 2
    inv_freq = ROPE_THETA ** (-jnp.arange(0, ROPE_DIM, 2, dtype=F32) / ROPE_DIM)
    lane = jnp.arange(LANES) % DIFF_DQK
    idx = jnp.where(lane < half, lane, jnp.where(lane < ROPE_DIM, lane - half, 0))
    invf = inv_freq[idx][None, :]
    pos = jnp.broadcast_to(positions.astype(F32)[:, None], (seq, LANES))
    table = jax.ShapeDtypeStruct((seq, LANES), F32)
    return pl.pallas_call(
        _rope_table_kernel,
        out_shape=(table, table, table),
        name="rope_tables",
    )(pos, invf)


def _inproj_kernel(x_ref, nw_ref, w_ref, wg_ref, c_ref, s1_ref, s2_ref, alog_ref, dtb_ref,
                   convw_ref, qkv_ref, z_ref, gate_ref, gate_t_ref, qk_ref, v_ref, pre_ref, *,
                   tiles_per_seq):
    hn = _rms(x_ref[...], nw_ref[...]).astype(BF16)
    tm = x_ref.shape[0]

    @pl.when((pl.program_id(0) % tiles_per_seq) == 0)
    def _():
        pre_ref[0:CONV_CARRY, :] = jnp.zeros((CONV_CARRY, DN_CONV), F32)

    def conv_blocks(first, last):
        for i in range(first, last):
            cols = slice(i * LANES, (i + 1) * LANES)
            cw = convw_ref[:, cols]
            for r0 in range(0, tm, CONV_ROWS):
                win = pre_ref[r0:r0 + CONV_CARRY + CONV_ROWS, cols]
                y = cw[CONV_WIDTH - 1:CONV_WIDTH, :] * win[CONV_CARRY:, :]
                for shift in range(1, CONV_WIDTH):
                    tap = CONV_WIDTH - 1 - shift
                    y = y + cw[tap:tap + 1, :] * pltpu.roll(win, shift, 0)[CONV_CARRY:, :]
                y = _silu(y)
                if i < 2 * DN_HEADS:
                    scale = DN_DK ** -0.5 if i < DN_HEADS else 1.0
                    y = y * (lax.rsqrt(jnp.sum(y * y, axis=-1, keepdims=True) + 1e-6) * scale)
                qkv_ref[r0:r0 + CONV_ROWS, cols] = y.astype(BF16)

    pre_ref[CONV_CARRY:CONV_CARRY + tm, :] = _dot(hn, w_ref[:, 0:DN_CONV])
    z_ref[...] = _dot(hn, w_ref[:, DN_CONV:DN_CONV + DN_V]).astype(BF16)
    conv_blocks(0, DN_HEADS)
    off = DN_CONV + DN_V
    qk = _dot(hn, w_ref[:, off:off + 2 * DF_QK])
    conv_blocks(DN_HEADS, 2 * DN_HEADS)
    c, s1, s2 = c_ref[...], s1_ref[...], s2_ref[...]
    half = ROPE_DIM // 2
    for i in range(2 * DF_QK // LANES):
        blk = qk[:, i * LANES:(i + 1) * LANES]
        rot = (blk * c + pltpu.roll(blk, half, 1) * s1 + pltpu.roll(blk, LANES - half, 1) * s2)
        qk_ref[:, i * LANES:(i + 1) * LANES] = rot.astype(BF16)
    off += 2 * DF_QK
    v_ref[...] = _dot(hn, w_ref[:, off:off + DF_V]).astype(BF16)
    conv_blocks(2 * DN_HEADS, 3 * DN_HEADS)
    pre_ref[0:CONV_CARRY, :] = pre_ref[tm:tm + CONV_CARRY, :]
    gpre = _dot(hn, wg_ref[...])
    lane = lax.broadcasted_iota(jnp.int32, gpre.shape, 1)
    pos_in_chunk = lax.broadcasted_iota(jnp.int32, gpre.shape, 0) & (CHUNK - 1)
    beta = 1.0 / (1.0 + jnp.exp(-gpre))
    t = gpre + dtb_ref[...]
    softplus = jnp.maximum(t, 0.0) + jnp.log1p(jnp.exp(-jnp.abs(t)))
    gcum = -jnp.exp(alog_ref[...]) * softplus
    shift = 1
    while shift < CHUNK:
        gcum = gcum + jnp.where(pos_in_chunk >= shift, pltpu.roll(gcum, shift, 0), 0.0)
        shift *= 2
    gates = jnp.where(lane < DN_HEADS, beta, gcum)
    gate_ref[...] = gates
    gate_t_ref[...] = gates.T[0:N_GATES, :]


def _input_projection(x2d, seq, norm_w, w_main, w_gate, tables, alog_row, dtb_row, conv_w):
    tokens = x2d.shape[0]
    tm = min(TOKEN_TILE, seq)
    tiles_per_seq = seq // tm
    row = lambda i: (i, 0)
    pos = lambda i: (i % tiles_per_seq, 0)
    out_shape = (
        jax.ShapeDtypeStruct((tokens, DN_CONV), BF16),
        jax.ShapeDtypeStruct((tokens, DN_V), BF16),
        jax.ShapeDtypeStruct((tokens, LANES), F32),
        jax.ShapeDtypeStruct((N_GATES, tokens), F32),
        jax.ShapeDtypeStruct((tokens, 2 * DF_QK), BF16),
        jax.ShapeDtypeStruct((tokens, DF_V), BF16),
    )
    return pl.pallas_call(
        functools.partial(_inproj_kernel, tiles_per_seq=tiles_per_seq),
        grid=(tokens // tm,),
        in_specs=[
            pl.BlockSpec((tm, D_MODEL), row),
            _resident((1, D_MODEL)),
            _resident((D_MODEL, MAIN_N)),
            _resident((D_MODEL, LANES)),
            pl.BlockSpec((tm, LANES), pos),
            pl.BlockSpec((tm, LANES), pos),
            pl.BlockSpec((tm, LANES), pos),
            _resident((1, LANES)),
            _resident((1, LANES)),
            _resident((CONV_WIDTH, DN_CONV)),
        ],
        out_specs=(
            pl.BlockSpec((tm, DN_CONV), row),
            pl.BlockSpec((tm, DN_V), row),
            pl.BlockSpec((tm, LANES), row),
            pl.BlockSpec((N_GATES, tm), lambda i: (0, i)),
            pl.BlockSpec((tm, 2 * DF_QK), row),
            pl.BlockSpec((tm, DF_V), row),
        ),
        out_shape=out_shape,
        scratch_shapes=[pltpu.VMEM((CONV_CARRY + tm, DN_CONV), F32)],
        compiler_params=pltpu.CompilerParams(
            dimension_semantics=("arbitrary",), vmem_limit_bytes=VMEM_LIMIT),
        name="input_projection",
    )(x2d, norm_w, w_main, w_gate, *tables, alog_row, dtb_row, conv_w)


def _deltanet_kernel(qkv_ref, z_ref, gate_ref, gate_t_ref, nw_ref, gs_ref, out_ref,
                     pq_s, n_s, o_s, dec_s, state_ref, *, seq):
    n_chunks = seq // CHUNK
    C = CHUNK
    ii = lax.broadcasted_iota(jnp.int32, (C, C), 0)
    jj = lax.broadcasted_iota(jnp.int32, (C, C), 1)
    eye = (ii == jj).astype(F32)
    lane = lax.broadcasted_iota(jnp.int32, (C, LANES), 1)

    def precompute(chains):
        st = []
        for c, h in chains:
            r0 = c * C if isinstance(c, int) else pl.multiple_of(c * C, C)
            rows = pl.ds(r0, C)
            qn, kn, v = (qkv_ref[rows, part * DN_QK + h * LANES:part * DN_QK + (h + 1) * LANES]
                         .astype(F32) for part in range(3))
            gates = gate_ref[rows, :]
            beta = jnp.sum(jnp.where(lane == h, gates, 0.0), axis=-1, keepdims=True)
            g_col = jnp.sum(jnp.where(lane == DN_HEADS + h, gates, 0.0), axis=-1, keepdims=True)
            g_row = gate_t_ref[DN_HEADS + h, pl.ds(c, 1), :]
            dec_incl = jnp.exp(jnp.where(ii >= jj, g_col - g_row, -1e30))
            g_last = g_col[C - 1:C, :]
            eg = jnp.exp(g_col)
            kb = kn * beta
            st.append(dict(
                c=c, h=h, rows=rows, dec_incl=dec_incl, qs=qn * eg, kb=kb, knb=kn.astype(BF16),
                lhs=jnp.concatenate([kb, qn], axis=0).astype(BF16),
                rhs=jnp.concatenate([kb * eg, v * beta], axis=1).astype(BF16),
                kd=(kn * jnp.exp(g_last - g_col)).astype(BF16),
                dec=jnp.broadcast_to(jnp.exp(g_last), (8, LANES))))
        for d in st:
            kq = _dot_nt(d["lhs"], d["knb"])
            d["qk"] = (kq[C:2 * C, :] * d["dec_incl"]).astype(BF16)
            d["n_pow"] = -(kq[0:C, :] * jnp.where(ii > jj, d["dec_incl"], 0.0))
            d["t_inv"] = eye + d["n_pow"]
        for _ in range(int(math.log2(C)) - 1):
            for d in st:
                nb = d["n_pow"].astype(BF16)
                d["n_pow"] = _dot(nb, nb)
            for d in st:
                d["t_inv"] = d["t_inv"] + _dot(d["t_inv"].astype(BF16), d["n_pow"].astype(BF16))
        for d in st:
            d["wu"] = _dot(d["t_inv"].astype(BF16), d["rhs"]).astype(BF16)
        for d in st:
            d["pn"] = _dot_tn(d["kd"], d["wu"])
            d["qo"] = _dot(d["qk"], d["wu"])
        for d in st:
            c, h, pn, qo = d["c"], d["h"], d["pn"], d["qo"]
            pq_s[h, c, 0:DN_DK, :] = pn[:, 0:DN_DV].astype(BF16)
            pq_s[h, c, DN_DK:DN_DK + C, :] = (d["qs"] - qo[:, 0:DN_DV]).astype(BF16)
            n_s[h, c] = pn[:, DN_DV:2 * DN_DV].astype(BF16)
            o_s[h, d["rows"], :] = qo[:, DN_DV:2 * DN_DV]
            dec_s[h, c] = d["dec"]

    nw = nw_ref[...]

    def recur(group):
        states = [state_ref[h] for h in range(DN_HEADS)]
        for sub in range(DN_UNROLL):
            c = group * DN_UNROLL + sub
            r0 = c * C if isinstance(c, int) else pl.multiple_of(c * C, C)
            rows = pl.ds(r0, C)
            for h in range(DN_HEADS):
                s = states[h]
                ps = _dot(pq_s[h, c], s.astype(BF16))
                states[h] = s * dec_s[h, c][0:1, :] + n_s[h, c].astype(F32) - ps[0:DN_DK, :]
                o = ps[DN_DK:DN_DK + C, :] + o_s[h, rows, :]
                z = z_ref[rows, h * LANES:(h + 1) * LANES].astype(F32)
                y = _rms(o, nw) * _silu(z) * gs_ref[:, h * LANES:(h + 1) * LANES]
                out_ref[rows, h * LANES:(h + 1) * LANES] = y.astype(BF16)
        for h in range(DN_HEADS):
            state_ref[h] = states[h]

    def group_chains(group):
        return [(group * DN_UNROLL + sub, h) for sub in range(DN_UNROLL) for h in range(DN_HEADS)]

    n_groups = n_chunks // DN_UNROLL
    state_ref[...] = jnp.zeros(state_ref.shape, F32)
    precompute(group_chains(0))

    def step(t, carry):
        recur(t - 1)
        precompute(group_chains(t))
        return carry

    lax.fori_loop(1, n_groups, step, 0)
    recur(n_groups - 1)


def _deltanet(qkv, z, gates, gates_t, norm_w, gs):
    batch, seq, _ = qkv.shape
    n_chunks = seq // CHUNK
    assert n_chunks % DN_UNROLL == 0
    per_batch = lambda width: pl.BlockSpec((None, seq, width), lambda b: (b, 0, 0))
    return pl.pallas_call(
        functools.partial(_deltanet_kernel, seq=seq),
        grid=(batch,),
        in_specs=[
            per_batch(DN_CONV),
            per_batch(DN_V),
            per_batch(LANES),
            pl.BlockSpec((N_GATES, n_chunks, CHUNK), lambda b: (0, b, 0)),
            _resident((1, DN_DV)),
            _resident((1, DN_V)),
        ],
        out_specs=per_batch(DN_V),
        out_shape=jax.ShapeDtypeStruct((batch, seq, DN_V), BF16),
        scratch_shapes=[
            pltpu.VMEM((DN_HEADS, n_chunks, DN_DK + CHUNK, DN_DV), BF16),
            pltpu.VMEM((DN_HEADS, n_chunks, DN_DK, DN_DV), BF16),
            pltpu.VMEM((DN_HEADS, seq, DN_DV), F32),
            pltpu.VMEM((DN_HEADS, n_chunks, 8, LANES), F32),
            pltpu.VMEM((DN_HEADS, DN_DK, DN_DV), F32),
        ],
        compiler_params=pltpu.CompilerParams(
            dimension_semantics=("arbitrary",), vmem_limit_bytes=VMEM_LIMIT),
        name="gated_deltanet",
    )(qkv, z, gates, gates_t, norm_w, gs)


def _diff_attn_kernel(lam_ref, nw_ref, gs_ref, q_ref, k_ref, v_ref, o_ref, vext_ref, *, seq, block,
                      lambda_init):
    lp = lam_ref[...]
    lam = (jnp.exp(jnp.sum(lp[0:1, :] * lp[1:2, :], axis=-1, keepdims=True))
           - jnp.exp(jnp.sum(lp[2:3, :] * lp[3:4, :], axis=-1, keepdims=True)) + lambda_init)
    lane = lax.broadcasted_iota(jnp.int32, (block, LANES), 1)
    first_map = lane < DIFF_DQK
    row = lax.broadcasted_iota(jnp.int32, (block, block), 0)
    col = lax.broadcasted_iota(jnp.int32, (block, block), 1)
    causal = row >= col
    half = block // 2
    scale = DIFF_DQK ** -0.5 * math.log2(math.e)
    zero = jnp.zeros((), BF16)

    vext_ref[:, 0:DIFF_DV] = v_ref[...]
    vext_ref[:, DIFF_DV:2 * DIFF_DV] = jnp.ones((seq, DIFF_DV), BF16)

    def scores(qs, keys, mask):
        kb = k_ref[keys, :]
        out = [_dot_nt(qm, kb) for qm in qs]
        if mask is not None:
            out = [jnp.where(mask, s, -1e30) for s in out]
        return out

    def absorb(carry, ss, keys):
        vext = vext_ref[keys, :]
        m_new = [jnp.maximum(m, jnp.max(s, axis=-1, keepdims=True)) for s, (m, _) in zip(ss, carry)]
        probs = [jnp.exp2((s - mn).astype(BF16)) for s, mn in zip(ss, m_new)]
        pv = [_dot(p, vext) for p in probs]
        return tuple((mn, jnp.exp2(m - mn) * acc + x) for mn, (m, acc), x in zip(m_new, carry, pv))

    def finish(qi, top, bottom):
        acc1, acc2 = (jnp.concatenate([t[1][0:half, :], b[1]], axis=0) for t, b in zip(top, bottom))
        a1, l1 = acc1[:, 0:DIFF_DV], acc1[:, DIFF_DV:2 * DIFF_DV]
        a2, l2 = acc2[:, 0:DIFF_DV], acc2[:, DIFF_DV:2 * DIFF_DV]
        o = a1 / l1 - lam * (a2 / l2)
        y = _rms(o, nw_ref[...]) * (1.0 - lambda_init) * gs_ref[...]
        o_ref[qi * block:(qi + 1) * block, :] = y.astype(BF16)

    n_blocks = seq // block
    per_block = []
    for qi in range(n_blocks):
        todo = [(qi, "all", slice(j * block, (j + 1) * block), None) for j in range(qi)]
        todo.append((qi, "all", slice(qi * block, qi * block + half), causal[:, 0:half]))
        todo.append((qi, "bottom", slice(qi * block + half, (qi + 1) * block),
                     causal[0:half, 0:half]))
        per_block.append(todo)
    items = []
    while any(per_block):
        for todo in reversed(per_block):
            if todo:
                items.append(todo.pop(0))

    queries = []
    for qi in range(n_blocks):
        q = (q_ref[qi * block:(qi + 1) * block, :].astype(F32) * scale).astype(BF16)
        queries.append((jnp.where(first_map, q, zero), jnp.where(first_map, zero, q)))
    init = (jnp.full((block, 1), -1e30, F32), jnp.zeros((block, 2 * DIFF_DV), F32))
    carries = [(init, init) for _ in range(n_blocks)]

    def item_scores(item):
        qi, part, keys, mask = item
        qs = queries[qi] if part == "all" else tuple(qm[half:, :] for qm in queries[qi])
        return scores(qs, keys, mask)

    ss = item_scores(items[0])
    for k, (qi, part, keys, _) in enumerate(items):
        ss_next = item_scores(items[k + 1]) if k + 1 < len(items) else None
        if part == "all":
            carries[qi] = absorb(carries[qi], ss, keys)
        else:
            bottom = absorb(tuple((m[half:, :], acc[half:, :]) for m, acc in carries[qi]), ss, keys)
            finish(qi, carries[qi], bottom)
        ss = ss_next


def _diff_attention(qk, v, lam_params, norm_w, gs, lambda_init):
    batch, seq, _ = v.shape
    block = min(ATTN_BLOCK, seq)
    head_cols = lambda off: pl.BlockSpec((None, seq, LANES), lambda b, h: (b, 0, off + h))
    return pl.pallas_call(
        functools.partial(_diff_attn_kernel, seq=seq, block=block, lambda_init=lambda_init),
        grid=(batch, DIFF_HEADS),
        in_specs=[
            _resident((4, DIFF_DQK)),
            _resident((1, DIFF_DV)),
            pl.BlockSpec((1, LANES), lambda b, h: (0, h)),
            head_cols(0),
            head_cols(DIFF_HEADS),
            head_cols(0),
        ],
        out_specs=head_cols(0),
        out_shape=jax.ShapeDtypeStruct((batch, seq, DF_V), BF16),
        scratch_shapes=[pltpu.VMEM((seq, 2 * DIFF_DV), BF16)],
        compiler_params=pltpu.CompilerParams(
            dimension_semantics=("arbitrary", "arbitrary"), vmem_limit_bytes=VMEM_LIMIT),
        name="diff_attention",
    )(lam_params, norm_w, gs, qk, qk, v)


def _out_mlp_kernel(x_ref, odn_ref, odf_ref, wo_ref, nw_ref, wu_ref, wd_ref, fw_ref, y_ref,
                    hidden_ref, *, final_norm):
    x1 = (x_ref[...] + _dot(odn_ref[...], wo_ref[0:DN_V, :])
          + _dot(odf_ref[...], wo_ref[DN_V:DN_V + DF_V, :]))
    hm = _rms(x1, nw_ref[...]).astype(BF16)
    for c in range(D_FF // FF_CHUNK):
        cols = slice(c * FF_CHUNK, (c + 1) * FF_CHUNK)
        hidden_ref[:, cols] = jnp.square(jnp.maximum(_dot(hm, wu_ref[:, cols]), 0.0)).astype(BF16)
    y = x1 + _dot(hidden_ref[...], wd_ref[...])
    if final_norm:
        y = _rms(y, fw_ref[...])
    y_ref[...] = y


def _out_mlp(x2d, o_dn, o_df, w_out, norm_w, w_up, w_down, final_w, final_norm):
    tokens = x2d.shape[0]
    tm = min(TOKEN_TILE, tokens)
    row = lambda i: (i, 0)
    return pl.pallas_call(
        functools.partial(_out_mlp_kernel, final_norm=final_norm),
        grid=(tokens // tm,),
        in_specs=[
            pl.BlockSpec((tm, D_MODEL), row),
            pl.BlockSpec((tm, DN_V), row),
            pl.BlockSpec((tm, DF_V), row),
            _resident((DN_V + DF_V, D_MODEL)),
            _resident((1, D_MODEL)),
            _resident((D_MODEL, D_FF)),
            _resident((D_FF, D_MODEL)),
            _resident((1, D_MODEL)),
        ],
        out_specs=pl.BlockSpec((tm, D_MODEL), row),
        out_shape=jax.ShapeDtypeStruct((tokens, D_MODEL), F32),
        scratch_shapes=[pltpu.VMEM((tm, D_FF), BF16)],
        compiler_params=pltpu.CompilerParams(
            dimension_semantics=("arbitrary",), vmem_limit_bytes=VMEM_LIMIT),
        name="out_projection_mlp",
    )(x2d, o_dn, o_df, w_out, norm_w, w_up, w_down, final_w)


def _pad_lanes(vec, offset):
    return jnp.zeros((1, LANES), F32).at[0, offset:offset + vec.shape[0]].set(vec.astype(F32))


def kernel(x, positions, attn_norm_w, w_in, conv_w, a_log, dt_bias, dn_norm_w, lambda_q1, lambda_k1, lambda_q2, lambda_k2, diff_norm_w, group_scale, w_out, mlp_norm_w, w_up, w_down, final_norm_w):
    batch, seq, _ = x.shape
    depth = w_in.shape[0]
    tables = _rope_tables(positions)
    x2d = x.reshape(batch * seq, D_MODEL)
    for l in range(depth):
        wl = w_in[l]
        w_main = jnp.concatenate([wl[:, :GATE_OFF], wl[:, GATE_OFF + N_GATES:]], axis=1).astype(BF16)
        w_gate = jnp.pad(wl[:, GATE_OFF:GATE_OFF + N_GATES], ((0, 0), (0, LANES - N_GATES))).astype(BF16)
        qkv, z, gates, gates_t, qk, v = _input_projection(
            x2d, seq, attn_norm_w[l][None, :], w_main, w_gate, tables,
            _pad_lanes(a_log[l], DN_HEADS), _pad_lanes(dt_bias[l], DN_HEADS), conv_w[l].astype(F32))
        gs = group_scale[l].astype(F32)[None, :]
        o_dn = _deltanet(qkv.reshape(batch, seq, DN_CONV), z.reshape(batch, seq, DN_V),
                         gates.reshape(batch, seq, LANES),
                         gates_t.reshape(N_GATES, batch * seq // CHUNK, CHUNK),
                         dn_norm_w[l].astype(F32)[None, :], gs[:, :DN_V])
        lam_params = jnp.stack([lambda_q1[l], lambda_k1[l], lambda_q2[l], lambda_k2[l]]).astype(F32)
        lambda_init = 0.8 - 0.6 * math.exp(-0.3 * l)
        o_df = _diff_attention(qk.reshape(batch, seq, 2 * DF_QK), v.reshape(batch, seq, DF_V),
                               lam_params, diff_norm_w[l].astype(F32)[None, :], gs[:, DN_V:],
                               lambda_init)
        x2d = _out_mlp(x2d, o_dn.reshape(batch * seq, DN_V), o_df.reshape(batch * seq, DF_V),
                       w_out[l].astype(BF16), mlp_norm_w[l][None, :], w_up[l].astype(BF16),
                       w_down[l].astype(BF16), final_norm_w[None, :], l == depth - 1)
    return x2d.reshape(batch, seq, D_MODEL)
```

```python
import functools
import math

import jax
import jax.numpy as jnp
from jax import lax
from jax.experimental import pallas as pl
from jax.experimental.pallas import tpu as pltpu

F32 = jnp.float32
BF16 = jnp.bfloat16

D_MODEL = 1024
DN_HEADS = 4
DN_DK = 128
DN_DV = 128
CONV_WIDTH = 4
CHUNK = 64
DIFF_HEADS = 4
DIFF_DQK = 64
DIFF_DV = 128
ROPE_THETA = 500000.0
ROPE_DIM = 16
D_FF = 4 * D_MODEL
EPS = 1e-6

DN_QK = DN_HEADS * DN_DK
DN_V = DN_HEADS * DN_DV
DN_CONV = 2 * DN_QK + DN_V
DF_QK = DIFF_HEADS * 2 * DIFF_DQK
DF_V = DIFF_HEADS * DIFF_DV
GATE_OFF = DN_CONV + DN_V
N_GATES = 2 * DN_HEADS
MAIN_N = DN_CONV + DN_V + 2 * DF_QK + DF_V
LANES = 128
CONV_CARRY = 8
CONV_ROWS = 64

VMEM_LIMIT = 56 * 1024 * 1024
TOKEN_TILE = 512
MLP_TILE = 1024
ATTN_BLOCK = 512
ATTN_HEADS_PER_STEP = 2
FF_CHUNK = 1024
DN_UNROLL = 8


def _dot(a, b):
    return jnp.dot(a, b, preferred_element_type=F32)


def _dot_nt(a, b):
    return lax.dot_general(a, b, (((1,), (1,)), ((), ())), preferred_element_type=F32)


def _dot_tn(a, b):
    return lax.dot_general(a, b, (((0,), (0,)), ((), ())), preferred_element_type=F32)


def _rms(x, w):
    return x * lax.rsqrt(jnp.mean(x * x, axis=-1, keepdims=True) + EPS) * w


def _silu(x):
    h = 0.5 * x
    return h + h * jnp.tanh(h)


def _resident(shape):
    zeros = (0,) * len(shape)
    return pl.BlockSpec(shape, lambda *_: zeros, pipeline_mode=pl.Buffered(1))


def _rope_table_kernel(pos_ref, invf_ref, c_ref, s1_ref, s2_ref):
    ang = pos_ref[...] * invf_ref[...]
    lane = lax.broadcasted_iota(jnp.int32, ang.shape, 1) & (DIFF_DQK - 1)
    c = jnp.cos(ang)
    s = jnp.sin(ang)
    half = ROPE_DIM // 2
    c_ref[...] = jnp.where(lane < ROPE_DIM, c, 1.0)
    s1_ref[...] = jnp.where((lane >= half) & (lane < ROPE_DIM), s, 0.0)
    s2_ref[...] = jnp.where(lane < half, -s, 0.0)


def _rope_tables(positions):
    seq = positions.shape[0]
    half = ROPE_DIM // 2
    inv_freq = ROPE_THETA ** (-jnp.arange(0, ROPE_DIM, 2, dtype=F32) / ROPE_DIM)
    lane = jnp.arange(LANES) % DIFF_DQK
    idx = jnp.where(lane < half, lane, jnp.where(lane < ROPE_DIM, lane - half, 0))
    invf = inv_freq[idx][None, :]
    pos = jnp.broadcast_to(positions.astype(F32)[:, None], (seq, LANES))
    table = jax.ShapeDtypeStruct((seq, LANES), F32)
    return pl.pallas_call(
        _rope_table_kernel,
        out_shape=(table, table, table),
        name="rope_tables",
    )(pos, invf)


def _inproj_kernel(x_ref, nw_ref, w_ref, wg_ref, c_ref, s1_ref, s2_ref, alog_ref, dtb_ref,
                   convw_ref, qkv_ref, z_ref, gate_ref, gate_t_ref, qk_ref, v_ref, pre_ref, *,
                   tiles_per_seq):
    hn = _rms(x_ref[...], nw_ref[...]).astype(BF16)
    tm = x_ref.shape[0]

    @pl.when((pl.program_id(0) % tiles_per_seq) == 0)
    def _():
        pre_ref[0:CONV_CARRY, :] = jnp.zeros((CONV_CARRY, DN_CONV), F32)

    def conv_blocks(first, last):
        for i in range(first, last):
            cols = slice(i * LANES, (i + 1) * LANES)
            cw = convw_ref[:, cols]
            for r0 in range(0, tm, CONV_ROWS):
                win = pre_ref[r0:r0 + CONV_CARRY + CONV_ROWS, cols]
                y = cw[CONV_WIDTH - 1:CONV_WIDTH, :] * win[CONV_CARRY:, :]
                for shift in range(1, CONV_WIDTH):
                    tap = CONV_WIDTH - 1 - shift
                    y = y + cw[tap:tap + 1, :] * pltpu.roll(win, shift, 0)[CONV_CARRY:, :]
                y = _silu(y)
                if i < 2 * DN_HEADS:
                    scale = DN_DK ** -0.5 if i < DN_HEADS else 1.0
                    y = y * (lax.rsqrt(jnp.sum(y * y, axis=-1, keepdims=True) + 1e-6) * scale)
                qkv_ref[r0:r0 + CONV_ROWS, cols] = y.astype(BF16)

    pre_ref[CONV_CARRY:CONV_CARRY + tm, :] = _dot(hn, w_ref[:, 0:DN_CONV])
    z_ref[...] = _dot(hn, w_ref[:, DN_CONV:DN_CONV + DN_V]).astype(BF16)
    conv_blocks(0, DN_HEADS)
    off = DN_CONV + DN_V
    qk = _dot(hn, w_ref[:, off:off + 2 * DF_QK])
    conv_blocks(DN_HEADS, 2 * DN_HEADS)
    c, s1, s2 = c_ref[...], s1_ref[...], s2_ref[...]
    half = ROPE_DIM // 2
    for i in range(2 * DF_QK // LANES):
        blk = qk[:, i * LANES:(i + 1) * LANES]
        rot = (blk * c + pltpu.roll(blk, half, 1) * s1 + pltpu.roll(blk, LANES - half, 1) * s2)
        qk_ref[:, i * LANES:(i + 1) * LANES] = rot.astype(BF16)
    off += 2 * DF_QK
    v_ref[...] = _dot(hn, w_ref[:, off:off + DF_V]).astype(BF16)
    conv_blocks(2 * DN_HEADS, 3 * DN_HEADS)
    pre_ref[0:CONV_CARRY, :] = pre_ref[tm:tm + CONV_CARRY, :]
    gpre = _dot(hn, wg_ref[...])
    lane = lax.broadcasted_iota(jnp.int32, gpre.shape, 1)
    pos_in_chunk = lax.broadcasted_iota(jnp.int32, gpre.shape, 0) & (CHUNK - 1)
    beta = 1.0 / (1.0 + jnp.exp(-gpre))
    t = gpre + dtb_ref[...]
    softplus = jnp.maximum(t, 0.0) + jnp.log1p(jnp.exp(-jnp.abs(t)))
    gcum = -jnp.exp(alog_ref[...]) * softplus
    shift = 1
    while shift < CHUNK:
        gcum = gcum + jnp.where(pos_in_chunk >= shift, pltpu.roll(gcum, shift, 0), 0.0)
        shift *= 2
    gates = jnp.where(lane < DN_HEADS, beta, gcum)
    gate_ref[...] = gates
    gate_t_ref[...] = gates.T[0:N_GATES, :]


def _input_projection(x2d, seq, norm_w, w_main, w_gate, tables, alog_row, dtb_row, conv_w):
    tokens = x2d.shape[0]
    tm = min(TOKEN_TILE, seq)
    tiles_per_seq = seq // tm
    row = lambda i: (i, 0)
    pos = lambda i: (i % tiles_per_seq, 0)
    out_shape = (
        jax.ShapeDtypeStruct((tokens, DN_CONV), BF16),
        jax.ShapeDtypeStruct((tokens, DN_V), BF16),
        jax.ShapeDtypeStruct((tokens, LANES), F32),
        jax.ShapeDtypeStruct((N_GATES, tokens), F32),
        jax.ShapeDtypeStruct((tokens, 2 * DF_QK), BF16),
        jax.ShapeDtypeStruct((tokens, DF_V), BF16),
    )
    return pl.pallas_call(
        functools.partial(_inproj_kernel, tiles_per_seq=tiles_per_seq),
        grid=(tokens // tm,),
        in_specs=[
            pl.BlockSpec((tm, D_MODEL), row),
            _resident((1, D_MODEL)),
            _resident((D_MODEL, MAIN_N)),
            _resident((D_MODEL, LANES)),
            pl.BlockSpec((tm, LANES), pos),
            pl.BlockSpec((tm, LANES), pos),
            pl.BlockSpec((tm, LANES), pos),
            _resident((1, LANES)),
            _resident((1, LANES)),
            _resident((CONV_WIDTH, DN_CONV)),
        ],
        out_specs=(
            pl.BlockSpec((tm, DN_CONV), row),
            pl.BlockSpec((tm, DN_V), row),
            pl.BlockSpec((tm, LANES), row),
            pl.BlockSpec((N_GATES, tm), lambda i: (0, i)),
            pl.BlockSpec((tm, 2 * DF_QK), row),
            pl.BlockSpec((tm, DF_V), row),
        ),
        out_shape=out_shape,
        scratch_shapes=[pltpu.VMEM((CONV_CARRY + tm, DN_CONV), F32)],
        compiler_params=pltpu.CompilerParams(
            dimension_semantics=("arbitrary",), vmem_limit_bytes=VMEM_LIMIT),
        name="input_projection",
    )(x2d, norm_w, w_main, w_gate, *tables, alog_row, dtb_row, conv_w)


def _deltanet_kernel(qkv_ref, z_ref, gate_ref, gate_t_ref, nw_ref, gs_ref, out_ref,
                     pq_s, n_s, o_s, dec_s, state_ref, *, seq):
    n_chunks = seq // CHUNK
    C = CHUNK
    ii = lax.broadcasted_iota(jnp.int32, (C, C), 0)
    jj = lax.broadcasted_iota(jnp.int32, (C, C), 1)
    eye = (ii == jj).astype(F32)
    lane = lax.broadcasted_iota(jnp.int32, (C, LANES), 1)

    def precompute(chains):
        st = []
        for c, h in chains:
            r0 = c * C if isinstance(c, int) else pl.multiple_of(c * C, C)
            rows = pl.ds(r0, C)
            qn, kn, v = (qkv_ref[rows, part * DN_QK + h * LANES:part * DN_QK + (h + 1) * LANES]
                         .astype(F32) for part in range(3))
            gates = gate_ref[rows, :]
            beta = jnp.sum(jnp.where(lane == h, gates, 0.0), axis=-1, keepdims=True)
            g_col = jnp.sum(jnp.where(lane == DN_HEADS + h, gates, 0.0), axis=-1, keepdims=True)
            g_row = gate_t_ref[DN_HEADS + h, pl.ds(c, 1), :]
            dec_incl = jnp.exp(jnp.where(ii >= jj, g_col - g_row, -1e30))
            g_last = g_col[C - 1:C, :]
            eg = jnp.exp(g_col)
            kb = kn * beta
            st.append(dict(
                c=c, h=h, rows=rows, dec_incl=dec_incl, qs=qn * eg, kb=kb, knb=kn.astype(BF16),
                lhs=jnp.concatenate([kb, qn], axis=0).astype(BF16),
                rhs=jnp.concatenate([kb * eg, v * beta], axis=1).astype(BF16),
                kd=(kn * jnp.exp(g_last - g_col)).astype(BF16),
                dec=jnp.broadcast_to(jnp.exp(g_last), (8, LANES))))
        for d in st:
            kq = _dot_nt(d["lhs"], d["knb"])
            d["qk"] = (kq[C:2 * C, :] * d["dec_incl"]).astype(BF16)
            d["n_pow"] = -(kq[0:C, :] * jnp.where(ii > jj, d["dec_incl"], 0.0))
            d["t_inv"] = eye + d["n_pow"]
        for _ in range(int(math.log2(C)) - 1):
            for d in st:
                nb = d["n_pow"].astype(BF16)
                d["n_pow"] = _dot(nb, nb)
            for d in st:
                d["t_inv"] = d["t_inv"] + _dot(d["t_inv"].astype(BF16), d["n_pow"].astype(BF16))
        for d in st:
            d["wu"] = _dot(d["t_inv"].astype(BF16), d["rhs"]).astype(BF16)
        for d in st:
            d["pn"] = _dot_tn(d["kd"], d["wu"])
            d["qo"] = _dot(d["qk"], d["wu"])
        for d in st:
            c, h, pn, qo = d["c"], d["h"], d["pn"], d["qo"]
            pq_s[h, c, 0:DN_DK, :] = pn[:, 0:DN_DV].astype(BF16)
            pq_s[h, c, DN_DK:DN_DK + C, :] = (d["qs"] - qo[:, 0:DN_DV]).astype(BF16)
            n_s[h, c] = pn[:, DN_DV:2 * DN_DV].astype(BF16)
            o_s[h, d["rows"], :] = qo[:, DN_DV:2 * DN_DV]
            dec_s[h, c] = d["dec"]

    nw = nw_ref[...]

    def recur(group):
        states = [state_ref[h] for h in range(DN_HEADS)]
        for sub in range(DN_UNROLL):
            c = group * DN_UNROLL + sub
            r0 = c * C if isinstance(c, int) else pl.multiple_of(c * C, C)
            rows = pl.ds(r0, C)
            for h in range(DN_HEADS):
                s = states[h]
                ps = _dot(pq_s[h, c], s.astype(BF16))
                states[h] = s * dec_s[h, c][0:1, :] + n_s[h, c].astype(F32) - ps[0:DN_DK, :]
                o = ps[DN_DK:DN_DK + C, :] + o_s[h, rows, :]
                z = z_ref[rows, h * LANES:(h + 1) * LANES].astype(F32)
                y = _rms(o, nw) * _silu(z) * gs_ref[:, h * LANES:(h + 1) * LANES]
                out_ref[rows, h * LANES:(h + 1) * LANES] = y.astype(BF16)
        for h in range(DN_HEADS):
            state_ref[h] = states[h]

    def group_chains(group):
        return [(group * DN_UNROLL + sub, h) for sub in range(DN_UNROLL) for h in range(DN_HEADS)]

    n_groups = n_chunks // DN_UNROLL
    state_ref[...] = jnp.zeros(state_ref.shape, F32)
    precompute(group_chains(0))

    def step(t, carry):
        recur(t - 1)
        precompute(group_chains(t))
        return carry

    lax.fori_loop(1, n_groups, step, 0)
    recur(n_groups - 1)


def _deltanet(qkv, z, gates, gates_t, norm_w, gs):
    batch, seq, _ = qkv.shape
    n_chunks = seq // CHUNK
    assert n_chunks % DN_UNROLL == 0
    per_batch = lambda width: pl.BlockSpec((None, seq, width), lambda b: (b, 0, 0))
    return pl.pallas_call(
        functools.partial(_deltanet_kernel, seq=seq),
        grid=(batch,),
        in_specs=[
            per_batch(DN_CONV),
            per_batch(DN_V),
            per_batch(LANES),
            pl.BlockSpec((N_GATES, n_chunks, CHUNK), lambda b: (0, b, 0)),
            _resident((1, DN_DV)),
            _resident((1, DN_V)),
        ],
        out_specs=per_batch(DN_V),
        out_shape=jax.ShapeDtypeStruct((batch, seq, DN_V), BF16),
        scratch_shapes=[
            pltpu.VMEM((DN_HEADS, n_chunks, DN_DK + CHUNK, DN_DV), BF16),
            pltpu.VMEM((DN_HEADS, n_chunks, DN_DK, DN_DV), BF16),
            pltpu.VMEM((DN_HEADS, seq, DN_DV), F32),
            pltpu.VMEM((DN_HEADS, n_chunks, 8, LANES), F32),
            pltpu.VMEM((DN_HEADS, DN_DK, DN_DV), F32),
        ],
        compiler_params=pltpu.CompilerParams(
            dimension_semantics=("arbitrary",), vmem_limit_bytes=VMEM_LIMIT),
        name="gated_deltanet",
    )(qkv, z, gates, gates_t, norm_w, gs)


def _diff_attn_kernel(lam_ref, nw_ref, gs_ref, q_ref, k_ref, v_ref, o_ref, vext_ref, *, seq, block,
                      heads, lambda_init):
    lp = lam_ref[...]
    lam = (jnp.exp(jnp.sum(lp[0:1, :] * lp[1:2, :], axis=-1, keepdims=True))
           - jnp.exp(jnp.sum(lp[2:3, :] * lp[3:4, :], axis=-1, keepdims=True)) + lambda_init)
    lane = lax.broadcasted_iota(jnp.int32, (block, LANES), 1)
    first_map = lane < DIFF_DQK
    row = lax.broadcasted_iota(jnp.int32, (block, block), 0)
    col = lax.broadcasted_iota(jnp.int32, (block, block), 1)
    causal = row >= col
    half = block // 2
    scale = DIFF_DQK ** -0.5 * math.log2(math.e)
    zero = jnp.zeros((), BF16)

    head_cols = [slice(hh * LANES, (hh + 1) * LANES) for hh in range(heads)]
    for hh in range(heads):
        vext_ref[hh, :, 0:DIFF_DV] = v_ref[:, head_cols[hh]]
        vext_ref[hh, :, DIFF_DV:2 * DIFF_DV] = jnp.ones((seq, DIFF_DV), BF16)

    def scores(qs, hh, keys, mask):
        kb = k_ref[keys, head_cols[hh]]
        out = [_dot_nt(qm, kb) for qm in qs]
        if mask is not None:
            out = [jnp.where(mask, s, -1e30) for s in out]
        return out

    def absorb(carry, ss, hh, keys):
        vext = vext_ref[hh, keys, :]
        m_new = [jnp.maximum(m, jnp.max(s, axis=-1, keepdims=True)) for s, (m, _) in zip(ss, carry)]
        probs = [jnp.exp2((s - mn).astype(BF16)) for s, mn in zip(ss, m_new)]
        pv = [_dot(p, vext) for p in probs]
        return tuple((mn, jnp.exp2(m - mn) * acc + x) for mn, (m, acc), x in zip(m_new, carry, pv))

    def finish(hh, qi, top, bottom):
        acc1, acc2 = (jnp.concatenate([t[1][0:half, :], b[1]], axis=0) for t, b in zip(top, bottom))
        a1, l1 = acc1[:, 0:DIFF_DV], acc1[:, DIFF_DV:2 * DIFF_DV]
        a2, l2 = acc2[:, 0:DIFF_DV], acc2[:, DIFF_DV:2 * DIFF_DV]
        o = a1 / l1 - lam * (a2 / l2)
        y = _rms(o, nw_ref[...]) * (1.0 - lambda_init) * gs_ref[:, head_cols[hh]]
        o_ref[qi * block:(qi + 1) * block, head_cols[hh]] = y.astype(BF16)

    n_blocks = seq // block
    streams = [(hh, qi) for hh in range(heads) for qi in range(n_blocks)]
    per_stream = []
    for sid, (hh, qi) in enumerate(streams):
        todo = [(sid, "all", slice(j * block, (j + 1) * block), None) for j in range(qi)]
        todo.append((sid, "all", slice(qi * block, qi * block + half), causal[:, 0:half]))
        todo.append((sid, "bottom", slice(qi * block + half, (qi + 1) * block),
                     causal[0:half, 0:half]))
        per_stream.append(todo)
    per_stream.sort(key=len, reverse=True)
    items = []
    while any(per_stream):
        for todo in per_stream:
            if todo:
                items.append(todo.pop(0))

    queries = []
    for hh, qi in streams:
        q = (q_ref[qi * block:(qi + 1) * block, head_cols[hh]].astype(F32) * scale).astype(BF16)
        queries.append((jnp.where(first_map, q, zero), jnp.where(first_map, zero, q)))
    init = (jnp.full((block, 1), -1e30, F32), jnp.zeros((block, 2 * DIFF_DV), F32))
    carries = [(init, init) for _ in streams]

    def item_scores(item):
        sid, part, keys, mask = item
        qs = queries[sid] if part == "all" else tuple(qm[half:, :] for qm in queries[sid])
        return scores(qs, streams[sid][0], keys, mask)

    ss = item_scores(items[0])
    for k, (sid, part, keys, _) in enumerate(items):
        hh, qi = streams[sid]
        ss_next = item_scores(items[k + 1]) if k + 1 < len(items) else None
        if part == "all":
            carries[sid] = absorb(carries[sid], ss, hh, keys)
        else:
            bottom = absorb(tuple((m[half:, :], acc[half:, :]) for m, acc in carries[sid]), ss, hh,
                            keys)
            finish(hh, qi, carries[sid], bottom)
        ss = ss_next


def _diff_attention(qk, v, lam_params, norm_w, gs, lambda_init):
    batch, seq, _ = v.shape
    block = min(ATTN_BLOCK, seq)
    heads = ATTN_HEADS_PER_STEP
    groups = DIFF_HEADS // heads
    head_cols = lambda off: pl.BlockSpec((None, seq, heads * LANES), lambda b, g: (b, 0, off + g))
    return pl.pallas_call(
        functools.partial(_diff_attn_kernel, seq=seq, block=block, heads=heads,
                          lambda_init=lambda_init),
        grid=(batch, groups),
        in_specs=[
            _resident((4, DIFF_DQK)),
            _resident((1, DIFF_DV)),
            pl.BlockSpec((1, heads * LANES), lambda b, g: (0, g)),
            head_cols(0),
            head_cols(groups),
            head_cols(0),
        ],
        out_specs=head_cols(0),
        out_shape=jax.ShapeDtypeStruct((batch, seq, DF_V), BF16),
        scratch_shapes=[pltpu.VMEM((heads, seq, 2 * DIFF_DV), BF16)],
        compiler_params=pltpu.CompilerParams(
            dimension_semantics=("arbitrary", "arbitrary"), vmem_limit_bytes=VMEM_LIMIT),
        name="diff_attention",
    )(lam_params, norm_w, gs, qk, qk, v)


def _out_mlp_kernel(x_ref, odn_ref, odf_ref, wo_ref, nw_ref, wu_ref, wd_ref, fw_ref, y_ref,
                    hidden_ref, *, final_norm):
    x1 = (x_ref[...] + _dot(odn_ref[...], wo_ref[0:DN_V, :])
          + _dot(odf_ref[...], wo_ref[DN_V:DN_V + DF_V, :]))
    hm = _rms(x1, nw_ref[...]).astype(BF16)
    for c in range(D_FF // FF_CHUNK):
        cols = slice(c * FF_CHUNK, (c + 1) * FF_CHUNK)
        hidden_ref[:, cols] = jnp.square(jnp.maximum(_dot(hm, wu_ref[:, cols]), 0.0)).astype(BF16)
    y = x1 + _dot(hidden_ref[...], wd_ref[...])
    if final_norm:
        y = _rms(y, fw_ref[...])
    y_ref[...] = y


def _out_mlp(x2d, o_dn, o_df, w_out, norm_w, w_up, w_down, final_w, final_norm):
    tokens = x2d.shape[0]
    tm = min(MLP_TILE, tokens)
    row = lambda i: (i, 0)
    return pl.pallas_call(
        functools.partial(_out_mlp_kernel, final_norm=final_norm),
        grid=(tokens // tm,),
        in_specs=[
            pl.BlockSpec((tm, D_MODEL), row),
            pl.BlockSpec((tm, DN_V), row),
            pl.BlockSpec((tm, DF_V), row),
            _resident((DN_V + DF_V, D_MODEL)),
            _resident((1, D_MODEL)),
            _resident((D_MODEL, D_FF)),
            _resident((D_FF, D_MODEL)),
            _resident((1, D_MODEL)),
        ],
        out_specs=pl.BlockSpec((tm, D_MODEL), row),
        out_shape=jax.ShapeDtypeStruct((tokens, D_MODEL), F32),
        scratch_shapes=[pltpu.VMEM((tm, D_FF), BF16)],
        compiler_params=pltpu.CompilerParams(
            dimension_semantics=("arbitrary",), vmem_limit_bytes=VMEM_LIMIT),
        name="out_projection_mlp",
    )(x2d, o_dn, o_df, w_out, norm_w, w_up, w_down, final_w)


def _pad_lanes(vec, offset):
    return jnp.zeros((1, LANES), F32).at[0, offset:offset + vec.shape[0]].set(vec.astype(F32))


def kernel(x, positions, attn_norm_w, w_in, conv_w, a_log, dt_bias, dn_norm_w, lambda_q1, lambda_k1, lambda_q2, lambda_k2, diff_norm_w, group_scale, w_out, mlp_norm_w, w_up, w_down, final_norm_w):
    batch, seq, _ = x.shape
    depth = w_in.shape[0]
    tables = _rope_tables(positions)
    x2d = x.reshape(batch * seq, D_MODEL)
    for l in range(depth):
        wl = w_in[l]
        w_main = jnp.concatenate([wl[:, :GATE_OFF], wl[:, GATE_OFF + N_GATES:]], axis=1).astype(BF16)
        w_gate = jnp.pad(wl[:, GATE_OFF:GATE_OFF + N_GATES], ((0, 0), (0, LANES - N_GATES))).astype(BF16)
        qkv, z, gates, gates_t, qk, v = _input_projection(
            x2d, seq, attn_norm_w[l][None, :], w_main, w_gate, tables,
            _pad_lanes(a_log[l], DN_HEADS), _pad_lanes(dt_bias[l], DN_HEADS), conv_w[l].astype(F32))
        gs = group_scale[l].astype(F32)[None, :]
        o_dn = _deltanet(qkv.reshape(batch, seq, DN_CONV), z.reshape(batch, seq, DN_V),
                         gates.reshape(batch, seq, LANES),
                         gates_t.reshape(N_GATES, batch * seq // CHUNK, CHUNK),
                         dn_norm_w[l].astype(F32)[None, :], gs[:, :DN_V])
        lam_params = jnp.stack([lambda_q1[l], lambda_k1[l], lambda_q2[l], lambda_k2[l]]).astype(F32)
        lambda_init = 0.8 - 0.6 * math.exp(-0.3 * l)
        o_df = _diff_attention(qk.reshape(batch, seq, 2 * DF_QK), v.reshape(batch, seq, DF_V),
                               lam_params, diff_norm_w[l].astype(F32)[None, :], gs[:, DN_V:],
                               lambda_init)
        x2d = _out_mlp(x2d, o_dn.reshape(batch * seq, DN_V), o_df.reshape(batch * seq, DF_V),
                       w_out[l].astype(BF16), mlp_norm_w[l][None, :], w_up[l].astype(BF16),
                       w_down[l].astype(BF16), final_norm_w[None, :], l == depth - 1)
    return x2d.reshape(batch, seq, D_MODEL)
```

```python
import functools
import math

import jax
import jax.numpy as jnp
from jax import lax
from jax.experimental import pallas as pl
from jax.experimental.pallas import tpu as pltpu

F32 = jnp.float32
BF16 = jnp.bfloat16

D_MODEL = 1024
DN_HEADS = 4
DN_DK = 128
DN_DV = 128
CONV_WIDTH = 4
CHUNK = 64
DIFF_HEADS = 4
DIFF_DQK = 64
DIFF_DV = 128
ROPE_THETA = 500000.0
ROPE_DIM = 16
D_FF = 4 * D_MODEL
EPS = 1e-6
L2_EPS = 1e-6
MASKED = -1e30

DN_QK = DN_HEADS * DN_DK
DN_V = DN_HEADS * DN_DV
DN_CONV = 2 * DN_QK + DN_V
DF_QK = DIFF_HEADS * 2 * DIFF_DQK
DF_V = DIFF_HEADS * DIFF_DV
GATE_OFF = DN_CONV + DN_V
N_GATES = 2 * DN_HEADS
MAIN_N = DN_CONV + DN_V + 2 * DF_QK + DF_V
LANES = 128
SUBLANES = 8
CONV_CARRY = 8
CONV_ROWS = 64

VMEM_LIMIT = 56 * 1024 * 1024
TOKEN_TILE = 512
MLP_TILE = 1024
ATTN_BLOCK = 512
ATTN_HEADS_PER_STEP = 2
FF_CHUNK = 1024
DN_UNROLL = 8


def _dot(a, b):
    return jnp.dot(a, b, preferred_element_type=F32)


def _dot_nt(a, b):
    return lax.dot_general(a, b, (((1,), (1,)), ((), ())), preferred_element_type=F32)


def _dot_tn(a, b):
    return lax.dot_general(a, b, (((0,), (0,)), ((), ())), preferred_element_type=F32)


def _rms(x, w):
    return x * lax.rsqrt(jnp.mean(x * x, axis=-1, keepdims=True) + EPS) * w


def _silu(x):
    h = 0.5 * x
    return h + h * jnp.tanh(h)


def _resident(shape):
    zeros = (0,) * len(shape)
    return pl.BlockSpec(shape, lambda *_: zeros, pipeline_mode=pl.Buffered(1))


def _rope_table_kernel(pos_ref, invf_ref, c_ref, s1_ref, s2_ref):
    ang = pos_ref[...] * invf_ref[...]
    lane = lax.broadcasted_iota(jnp.int32, ang.shape, 1) & (DIFF_DQK - 1)
    c = jnp.cos(ang)
    s = jnp.sin(ang)
    half = ROPE_DIM // 2
    c_ref[...] = jnp.where(lane < ROPE_DIM, c, 1.0)
    s1_ref[...] = jnp.where((lane >= half) & (lane < ROPE_DIM), s, 0.0)
    s2_ref[...] = jnp.where(lane < half, -s, 0.0)


def _rope_tables(positions):
    seq = positions.shape[0]
    half = ROPE_DIM // 2
    inv_freq = ROPE_THETA ** (-jnp.arange(0, ROPE_DIM, 2, dtype=F32) / ROPE_DIM)
    lane = jnp.arange(LANES) % DIFF_DQK
    idx = jnp.where(lane < half, lane, jnp.where(lane < ROPE_DIM, lane - half, 0))
    invf = inv_freq[idx][None, :]
    pos = jnp.broadcast_to(positions.astype(F32)[:, None], (seq, LANES))
    table = jax.ShapeDtypeStruct((seq, LANES), F32)
    return pl.pallas_call(
        _rope_table_kernel,
        out_shape=(table, table, table),
        name="rope_tables",
    )(pos, invf)


def _inproj_kernel(x_ref, nw_ref, w_ref, wg_ref, c_ref, s1_ref, s2_ref, alog_ref, dtb_ref,
                   convw_ref, qkv_ref, z_ref, gate_ref, gate_t_ref, qk_ref, v_ref, pre_ref, *,
                   tiles_per_seq):
    hn = _rms(x_ref[...], nw_ref[...]).astype(BF16)
    tm = x_ref.shape[0]

    @pl.when((pl.program_id(0) % tiles_per_seq) == 0)
    def _():
        pre_ref[0:CONV_CARRY, :] = jnp.zeros((CONV_CARRY, DN_CONV), F32)

    def conv_blocks(first, last):
        for i in range(first, last):
            cols = slice(i * LANES, (i + 1) * LANES)
            cw = convw_ref[:, cols]
            for r0 in range(0, tm, CONV_ROWS):
                win = pre_ref[r0:r0 + CONV_CARRY + CONV_ROWS, cols]
                y = cw[CONV_WIDTH - 1:CONV_WIDTH, :] * win[CONV_CARRY:, :]
                for shift in range(1, CONV_WIDTH):
                    tap = CONV_WIDTH - 1 - shift
                    y = y + cw[tap:tap + 1, :] * pltpu.roll(win, shift, 0)[CONV_CARRY:, :]
                y = _silu(y)
                if i < 2 * DN_HEADS:
                    scale = DN_DK ** -0.5 if i < DN_HEADS else 1.0
                    y = y * (lax.rsqrt(jnp.sum(y * y, axis=-1, keepdims=True) + L2_EPS) * scale)
                qkv_ref[r0:r0 + CONV_ROWS, cols] = y.astype(BF16)

    pre_ref[CONV_CARRY:CONV_CARRY + tm, :] = _dot(hn, w_ref[:, 0:DN_CONV])
    z_ref[...] = _dot(hn, w_ref[:, DN_CONV:DN_CONV + DN_V]).astype(BF16)
    conv_blocks(0, DN_HEADS)
    off = DN_CONV + DN_V
    qk = _dot(hn, w_ref[:, off:off + 2 * DF_QK])
    conv_blocks(DN_HEADS, 2 * DN_HEADS)
    c, s1, s2 = c_ref[...], s1_ref[...], s2_ref[...]
    half = ROPE_DIM // 2
    for i in range(2 * DF_QK // LANES):
        blk = qk[:, i * LANES:(i + 1) * LANES]
        rot = (blk * c + pltpu.roll(blk, half, 1) * s1 + pltpu.roll(blk, LANES - half, 1) * s2)
        qk_ref[:, i * LANES:(i + 1) * LANES] = rot.astype(BF16)
    off += 2 * DF_QK
    v_ref[...] = _dot(hn, w_ref[:, off:off + DF_V]).astype(BF16)
    conv_blocks(2 * DN_HEADS, 3 * DN_HEADS)
    pre_ref[0:CONV_CARRY, :] = pre_ref[tm:tm + CONV_CARRY, :]
    gpre = _dot(hn, wg_ref[...])
    lane = lax.broadcasted_iota(jnp.int32, gpre.shape, 1)
    pos_in_chunk = lax.broadcasted_iota(jnp.int32, gpre.shape, 0) & (CHUNK - 1)
    beta = 1.0 / (1.0 + jnp.exp(-gpre))
    t = gpre + dtb_ref[...]
    softplus = jnp.maximum(t, 0.0) + jnp.log1p(jnp.exp(-jnp.abs(t)))
    gcum = -jnp.exp(alog_ref[...]) * softplus
    shift = 1
    while shift < CHUNK:
        gcum = gcum + jnp.where(pos_in_chunk >= shift, pltpu.roll(gcum, shift, 0), 0.0)
        shift *= 2
    gates = jnp.where(lane < DN_HEADS, beta, gcum)
    gate_ref[...] = gates
    gate_t_ref[...] = gates.T[0:N_GATES, :]


def _input_projection(x2d, seq, norm_w, w_main, w_gate, tables, alog_row, dtb_row, conv_w):
    tokens = x2d.shape[0]
    tm = min(TOKEN_TILE, seq)
    tiles_per_seq = seq // tm
    row = lambda i: (i, 0)
    pos = lambda i: (i % tiles_per_seq, 0)
    out_shape = (
        jax.ShapeDtypeStruct((tokens, DN_CONV), BF16),
        jax.ShapeDtypeStruct((tokens, DN_V), BF16),
        jax.ShapeDtypeStruct((tokens, LANES), F32),
        jax.ShapeDtypeStruct((N_GATES, tokens), F32),
        jax.ShapeDtypeStruct((tokens, 2 * DF_QK), BF16),
        jax.ShapeDtypeStruct((tokens, DF_V), BF16),
    )
    return pl.pallas_call(
        functools.partial(_inproj_kernel, tiles_per_seq=tiles_per_seq),
        grid=(tokens // tm,),
        in_specs=[
            pl.BlockSpec((tm, D_MODEL), row),
            _resident((1, D_MODEL)),
            _resident((D_MODEL, MAIN_N)),
            _resident((D_MODEL, LANES)),
            pl.BlockSpec((tm, LANES), pos),
            pl.BlockSpec((tm, LANES), pos),
            pl.BlockSpec((tm, LANES), pos),
            _resident((1, LANES)),
            _resident((1, LANES)),
            _resident((CONV_WIDTH, DN_CONV)),
        ],
        out_specs=(
            pl.BlockSpec((tm, DN_CONV), row),
            pl.BlockSpec((tm, DN_V), row),
            pl.BlockSpec((tm, LANES), row),
            pl.BlockSpec((N_GATES, tm), lambda i: (0, i)),
            pl.BlockSpec((tm, 2 * DF_QK), row),
            pl.BlockSpec((tm, DF_V), row),
        ),
        out_shape=out_shape,
        scratch_shapes=[pltpu.VMEM((CONV_CARRY + tm, DN_CONV), F32)],
        compiler_params=pltpu.CompilerParams(
            dimension_semantics=("arbitrary",), vmem_limit_bytes=VMEM_LIMIT),
        name="input_projection",
    )(x2d, norm_w, w_main, w_gate, *tables, alog_row, dtb_row, conv_w)


def _deltanet_kernel(qkv_ref, z_ref, gate_ref, gate_t_ref, nw_ref, gs_ref, out_ref,
                     pq_s, n_s, o_s, dec_s, state_ref, *, seq):
    n_chunks = seq // CHUNK
    C = CHUNK
    ii = lax.broadcasted_iota(jnp.int32, (C, C), 0)
    jj = lax.broadcasted_iota(jnp.int32, (C, C), 1)
    eye = (ii == jj).astype(F32)
    lane = lax.broadcasted_iota(jnp.int32, (C, LANES), 1)

    def precompute(chains):
        st = []
        for c, h in chains:
            r0 = c * C if isinstance(c, int) else pl.multiple_of(c * C, C)
            rows = pl.ds(r0, C)
            qn, kn, v = (qkv_ref[rows, part * DN_QK + h * LANES:part * DN_QK + (h + 1) * LANES]
                         .astype(F32) for part in range(3))
            gates = gate_ref[rows, :]
            beta = jnp.sum(jnp.where(lane == h, gates, 0.0), axis=-1, keepdims=True)
            g_col = jnp.sum(jnp.where(lane == DN_HEADS + h, gates, 0.0), axis=-1, keepdims=True)
            g_row = gate_t_ref[DN_HEADS + h, pl.ds(c, 1), :]
            dec_incl = jnp.exp(jnp.where(ii >= jj, g_col - g_row, MASKED))
            g_last = g_col[C - 1:C, :]
            eg = jnp.exp(g_col)
            kb = kn * beta
            st.append(dict(
                c=c, h=h, rows=rows, dec_incl=dec_incl, qs=qn * eg, kb=kb, knb=kn.astype(BF16),
                lhs=jnp.concatenate([kb, qn], axis=0).astype(BF16),
                rhs=jnp.concatenate([kb * eg, v * beta], axis=1).astype(BF16),
                kd=(kn * jnp.exp(g_last - g_col)).astype(BF16),
                dec=jnp.broadcast_to(jnp.exp(g_last), (SUBLANES, LANES))))
        for d in st:
            kq = _dot_nt(d["lhs"], d["knb"])
            d["qk"] = (kq[C:2 * C, :] * d["dec_incl"]).astype(BF16)
            d["n_pow"] = -(kq[0:C, :] * jnp.where(ii > jj, d["dec_incl"], 0.0))
            d["t_inv"] = eye + d["n_pow"]
        for _ in range(int(math.log2(C)) - 1):
            for d in st:
                nb = d["n_pow"].astype(BF16)
                d["n_pow"] = _dot(nb, nb)
            for d in st:
                d["t_inv"] = d["t_inv"] + _dot(d["t_inv"].astype(BF16), d["n_pow"].astype(BF16))
        for d in st:
            d["wu"] = _dot(d["t_inv"].astype(BF16), d["rhs"]).astype(BF16)
        for d in st:
            d["pn"] = _dot_tn(d["kd"], d["wu"])
            d["qo"] = _dot(d["qk"], d["wu"])
        for d in st:
            c, h, pn, qo = d["c"], d["h"], d["pn"], d["qo"]
            pq_s[h, c, 0:DN_DK, :] = pn[:, 0:DN_DV].astype(BF16)
            pq_s[h, c, DN_DK:DN_DK + C, :] = (d["qs"] - qo[:, 0:DN_DV]).astype(BF16)
            n_s[h, c] = pn[:, DN_DV:2 * DN_DV].astype(BF16)
            o_s[h, d["rows"], :] = qo[:, DN_DV:2 * DN_DV]
            dec_s[h, c] = d["dec"]

    nw = nw_ref[...]

    def recur(group):
        states = [state_ref[h] for h in range(DN_HEADS)]
        for sub in range(DN_UNROLL):
            c = group * DN_UNROLL + sub
            r0 = c * C if isinstance(c, int) else pl.multiple_of(c * C, C)
            rows = pl.ds(r0, C)
            for h in range(DN_HEADS):
                s = states[h]
                ps = _dot(pq_s[h, c], s.astype(BF16))
                states[h] = s * dec_s[h, c][0:1, :] + n_s[h, c].astype(F32) - ps[0:DN_DK, :]
                o = ps[DN_DK:DN_DK + C, :] + o_s[h, rows, :]
                z = z_ref[rows, h * LANES:(h + 1) * LANES].astype(F32)
                y = _rms(o, nw) * _silu(z) * gs_ref[:, h * LANES:(h + 1) * LANES]
                out_ref[rows, h * LANES:(h + 1) * LANES] = y.astype(BF16)
        for h in range(DN_HEADS):
            state_ref[h] = states[h]

    def group_chains(group):
        return [(group * DN_UNROLL + sub, h) for sub in range(DN_UNROLL) for h in range(DN_HEADS)]

    n_groups = n_chunks // DN_UNROLL
    state_ref[...] = jnp.zeros(state_ref.shape, F32)
    precompute(group_chains(0))

    def step(t, carry):
        recur(t - 1)
        precompute(group_chains(t))
        return carry

    lax.fori_loop(1, n_groups, step, 0)
    recur(n_groups - 1)


def _deltanet(qkv, z, gates, gates_t, norm_w, gs):
    batch, seq, _ = qkv.shape
    n_chunks = seq // CHUNK
    assert n_chunks % DN_UNROLL == 0
    per_batch = lambda width: pl.BlockSpec((None, seq, width), lambda b: (b, 0, 0))
    return pl.pallas_call(
        functools.partial(_deltanet_kernel, seq=seq),
        grid=(batch,),
        in_specs=[
            per_batch(DN_CONV),
            per_batch(DN_V),
            per_batch(LANES),
            pl.BlockSpec((N_GATES, n_chunks, CHUNK), lambda b: (0, b, 0)),
            _resident((1, DN_DV)),
            _resident((1, DN_V)),
        ],
        out_specs=per_batch(DN_V),
        out_shape=jax.ShapeDtypeStruct((batch, seq, DN_V), BF16),
        scratch_shapes=[
            pltpu.VMEM((DN_HEADS, n_chunks, DN_DK + CHUNK, DN_DV), BF16),
            pltpu.VMEM((DN_HEADS, n_chunks, DN_DK, DN_DV), BF16),
            pltpu.VMEM((DN_HEADS, seq, DN_DV), F32),
            pltpu.VMEM((DN_HEADS, n_chunks, SUBLANES, LANES), F32),
            pltpu.VMEM((DN_HEADS, DN_DK, DN_DV), F32),
        ],
        compiler_params=pltpu.CompilerParams(
            dimension_semantics=("arbitrary",), vmem_limit_bytes=VMEM_LIMIT),
        name="gated_deltanet",
    )(qkv, z, gates, gates_t, norm_w, gs)


def _diff_attn_kernel(lam_ref, nw_ref, gs_ref, q_ref, k_ref, v_ref, o_ref, vext_ref, *, seq, block,
                      heads, lambda_init):
    lp = lam_ref[...]
    lam = (jnp.exp(jnp.sum(lp[0:1, :] * lp[1:2, :], axis=-1, keepdims=True))
           - jnp.exp(jnp.sum(lp[2:3, :] * lp[3:4, :], axis=-1, keepdims=True)) + lambda_init)
    lane = lax.broadcasted_iota(jnp.int32, (block, LANES), 1)
    first_map = lane < DIFF_DQK
    row = lax.broadcasted_iota(jnp.int32, (block, block), 0)
    col = lax.broadcasted_iota(jnp.int32, (block, block), 1)
    causal = row >= col
    half = block // 2
    scale = DIFF_DQK ** -0.5 * math.log2(math.e)
    zero = jnp.zeros((), BF16)

    head_cols = [slice(hh * LANES, (hh + 1) * LANES) for hh in range(heads)]
    for hh in range(heads):
        vext_ref[hh, :, 0:DIFF_DV] = v_ref[:, head_cols[hh]]
        vext_ref[hh, :, DIFF_DV:2 * DIFF_DV] = jnp.ones((seq, DIFF_DV), BF16)

    def scores(qs, hh, keys, mask):
        kb = k_ref[keys, head_cols[hh]]
        out = [_dot_nt(qm, kb) for qm in qs]
        if mask is not None:
            out = [jnp.where(mask, s, MASKED) for s in out]
        return out

    def absorb(carry, ss, hh, keys):
        vext = vext_ref[hh, keys, :]
        m_new = [jnp.maximum(m, jnp.max(s, axis=-1, keepdims=True)) for s, (m, _) in zip(ss, carry)]
        probs = [jnp.exp2((s - mn).astype(BF16)) for s, mn in zip(ss, m_new)]
        pv = [_dot(p, vext) for p in probs]
        return tuple((mn, jnp.exp2(m - mn) * acc + x) for mn, (m, acc), x in zip(m_new, carry, pv))

    def finish(hh, qi, top, bottom):
        acc1, acc2 = (jnp.concatenate([t[1][0:half, :], b[1]], axis=0) for t, b in zip(top, bottom))
        a1, l1 = acc1[:, 0:DIFF_DV], acc1[:, DIFF_DV:2 * DIFF_DV]
        a2, l2 = acc2[:, 0:DIFF_DV], acc2[:, DIFF_DV:2 * DIFF_DV]
        o = a1 / l1 - lam * (a2 / l2)
        y = _rms(o, nw_ref[...]) * (1.0 - lambda_init) * gs_ref[:, head_cols[hh]]
        o_ref[qi * block:(qi + 1) * block, head_cols[hh]] = y.astype(BF16)

    n_blocks = seq // block
    streams = [(hh, qi) for hh in range(heads) for qi in range(n_blocks)]
    per_stream = []
    for sid, (hh, qi) in enumerate(streams):
        todo = [(sid, "all", slice(j * block, (j + 1) * block), None) for j in range(qi)]
        todo.append((sid, "all", slice(qi * block, qi * block + half), causal[:, 0:half]))
        todo.append((sid, "bottom", slice(qi * block + half, (qi + 1) * block),
                     causal[0:half, 0:half]))
        per_stream.append(todo)
    per_stream.sort(key=len, reverse=True)
    items = []
    while any(per_stream):
        for todo in per_stream:
            if todo:
                items.append(todo.pop(0))

    queries = []
    for hh, qi in streams:
        q = (q_ref[qi * block:(qi + 1) * block, head_cols[hh]].astype(F32) * scale).astype(BF16)
        queries.append((jnp.where(first_map, q, zero), jnp.where(first_map, zero, q)))
    init = (jnp.full((block, 1), MASKED, F32), jnp.zeros((block, 2 * DIFF_DV), F32))
    carries = [(init, init) for _ in streams]

    def item_scores(item):
        sid, part, keys, mask = item
        qs = queries[sid] if part == "all" else tuple(qm[half:, :] for qm in queries[sid])
        return scores(qs, streams[sid][0], keys, mask)

    ss = item_scores(items[0])
    for k, (sid, part, keys, _) in enumerate(items):
        hh, qi = streams[sid]
        ss_next = item_scores(items[k + 1]) if k + 1 < len(items) else None
        if part == "all":
            carries[sid] = absorb(carries[sid], ss, hh, keys)
        else:
            bottom = absorb(tuple((m[half:, :], acc[half:, :]) for m, acc in carries[sid]), ss, hh,
                            keys)
            finish(hh, qi, carries[sid], bottom)
        ss = ss_next


def _diff_attention(qk, v, lam_params, norm_w, gs, lambda_init):
    batch, seq, _ = v.shape
    block = min(ATTN_BLOCK, seq)
    heads = ATTN_HEADS_PER_STEP
    groups = DIFF_HEADS // heads
    head_cols = lambda off: pl.BlockSpec((None, seq, heads * LANES), lambda b, g: (b, 0, off + g))
    return pl.pallas_call(
        functools.partial(_diff_attn_kernel, seq=seq, block=block, heads=heads,
                          lambda_init=lambda_init),
        grid=(batch, groups),
        in_specs=[
            _resident((4, DIFF_DQK)),
            _resident((1, DIFF_DV)),
            pl.BlockSpec((1, heads * LANES), lambda b, g: (0, g)),
            head_cols(0),
            head_cols(groups),
            head_cols(0),
        ],
        out_specs=head_cols(0),
        out_shape=jax.ShapeDtypeStruct((batch, seq, DF_V), BF16),
        scratch_shapes=[pltpu.VMEM((heads, seq, 2 * DIFF_DV), BF16)],
        compiler_params=pltpu.CompilerParams(
            dimension_semantics=("arbitrary", "arbitrary"), vmem_limit_bytes=VMEM_LIMIT),
        name="diff_attention",
    )(lam_params, norm_w, gs, qk, qk, v)


def _out_mlp_kernel(x_ref, odn_ref, odf_ref, wo_ref, nw_ref, wu_ref, wd_ref, fw_ref, y_ref,
                    hidden_ref, *, final_norm):
    x1 = (x_ref[...] + _dot(odn_ref[...], wo_ref[0:DN_V, :])
          + _dot(odf_ref[...], wo_ref[DN_V:DN_V + DF_V, :]))
    hm = _rms(x1, nw_ref[...]).astype(BF16)
    for c in range(D_FF // FF_CHUNK):
        cols = slice(c * FF_CHUNK, (c + 1) * FF_CHUNK)
        hidden_ref[:, cols] = jnp.square(jnp.maximum(_dot(hm, wu_ref[:, cols]), 0.0)).astype(BF16)
    y = x1 + _dot(hidden_ref[...], wd_ref[...])
    if final_norm:
        y = _rms(y, fw_ref[...])
    y_ref[...] = y


def _out_mlp(x2d, o_dn, o_df, w_out, norm_w, w_up, w_down, final_w, final_norm):
    tokens = x2d.shape[0]
    tm = min(MLP_TILE, tokens)
    row = lambda i: (i, 0)
    return pl.pallas_call(
        functools.partial(_out_mlp_kernel, final_norm=final_norm),
        grid=(tokens // tm,),
        in_specs=[
            pl.BlockSpec((tm, D_MODEL), row),
            pl.BlockSpec((tm, DN_V), row),
            pl.BlockSpec((tm, DF_V), row),
            _resident((DN_V + DF_V, D_MODEL)),
            _resident((1, D_MODEL)),
            _resident((D_MODEL, D_FF)),
            _resident((D_FF, D_MODEL)),
            _resident((1, D_MODEL)),
        ],
        out_specs=pl.BlockSpec((tm, D_MODEL), row),
        out_shape=jax.ShapeDtypeStruct((tokens, D_MODEL), F32),
        scratch_shapes=[pltpu.VMEM((tm, D_FF), BF16)],
        compiler_params=pltpu.CompilerParams(
            dimension_semantics=("arbitrary",), vmem_limit_bytes=VMEM_LIMIT),
        name="out_projection_mlp",
    )(x2d, o_dn, o_df, w_out, norm_w, w_up, w_down, final_w)


def _pad_lanes(vec, offset):
    return jnp.zeros((1, LANES), F32).at[0, offset:offset + vec.shape[0]].set(vec.astype(F32))


def kernel(x, positions, attn_norm_w, w_in, conv_w, a_log, dt_bias, dn_norm_w, lambda_q1, lambda_k1, lambda_q2, lambda_k2, diff_norm_w, group_scale, w_out, mlp_norm_w, w_up, w_down, final_norm_w):
    batch, seq, _ = x.shape
    depth = w_in.shape[0]
    tables = _rope_tables(positions)
    x2d = x.reshape(batch * seq, D_MODEL)
    for l in range(depth):
        wl = w_in[l]
        w_main = jnp.concatenate([wl[:, :GATE_OFF], wl[:, GATE_OFF + N_GATES:]], axis=1).astype(BF16)
        w_gate = jnp.pad(wl[:, GATE_OFF:GATE_OFF + N_GATES], ((0, 0), (0, LANES - N_GATES))).astype(BF16)
        qkv, z, gates, gates_t, qk, v = _input_projection(
            x2d, seq, attn_norm_w[l][None, :], w_main, w_gate, tables,
            _pad_lanes(a_log[l], DN_HEADS), _pad_lanes(dt_bias[l], DN_HEADS), conv_w[l].astype(F32))
        gs = group_scale[l].astype(F32)[None, :]
        o_dn = _deltanet(qkv.reshape(batch, seq, DN_CONV), z.reshape(batch, seq, DN_V),
                         gates.reshape(batch, seq, LANES),
                         gates_t.reshape(N_GATES, batch * seq // CHUNK, CHUNK),
                         dn_norm_w[l].astype(F32)[None, :], gs[:, :DN_V])
        lam_params = jnp.stack([lambda_q1[l], lambda_k1[l], lambda_q2[l], lambda_k2[l]]).astype(F32)
        lambda_init = 0.8 - 0.6 * math.exp(-0.3 * l)
        o_df = _diff_attention(qk.reshape(batch, seq, 2 * DF_QK), v.reshape(batch, seq, DF_V),
                               lam_params, diff_norm_w[l].astype(F32)[None, :], gs[:, DN_V:],
                               lambda_init)
        x2d = _out_mlp(x2d, o_dn.reshape(batch * seq, DN_V), o_df.reshape(batch * seq, DF_V),
                       w_out[l].astype(BF16), mlp_norm_w[l][None, :], w_up[l].astype(BF16),
                       w_down[l].astype(BF16), final_norm_w[None, :], l == depth - 1)
    return x2d.reshape(batch, seq, D_MODEL)
```

```python
import functools
import math

import jax
import jax.numpy as jnp
from jax import lax
from jax.experimental import pallas as pl
from jax.experimental.pallas import tpu as pltpu

F32 = jnp.float32
BF16 = jnp.bfloat16

D_MODEL = 1024
DN_HEADS = 4
DN_DK = 128
DN_DV = 128
CONV_WIDTH = 4
CHUNK = 64
DIFF_HEADS = 4
DIFF_DQK = 64
DIFF_DV = 128
ROPE_THETA = 500000.0
ROPE_DIM = 16
D_FF = 4 * D_MODEL
EPS = 1e-6
L2_EPS = 1e-6
MASKED = -1e30

DN_QK = DN_HEADS * DN_DK
DN_V = DN_HEADS * DN_DV
DN_CONV = 2 * DN_QK + DN_V
DF_QK = DIFF_HEADS * 2 * DIFF_DQK
DF_V = DIFF_HEADS * DIFF_DV
GATE_OFF = DN_CONV + DN_V
N_GATES = 2 * DN_HEADS
LANES = 128
SUBLANES = 8
CONV_CARRY = 8
CONV_ROWS = 64

VMEM_LIMIT = 56 * 1024 * 1024
TOKEN_TILE = 512
MLP_TILE = 1024
ATTN_BLOCK = 512
ATTN_HEADS_PER_STEP = 2
FF_CHUNK = 1024
DN_UNROLL = 8


def _dot(a, b):
    return jnp.dot(a, b, preferred_element_type=F32)


def _dot_nt(a, b):
    return lax.dot_general(a, b, (((1,), (1,)), ((), ())), preferred_element_type=F32)


def _dot_tn(a, b):
    return lax.dot_general(a, b, (((0,), (0,)), ((), ())), preferred_element_type=F32)


def _rms(x, w):
    return x * lax.rsqrt(jnp.mean(x * x, axis=-1, keepdims=True) + EPS) * w


def _silu(x):
    h = 0.5 * x
    return h + h * jnp.tanh(h)


def _resident(shape):
    zeros = (0,) * len(shape)
    return pl.BlockSpec(shape, lambda *_: zeros, pipeline_mode=pl.Buffered(1))


def _rope_table_kernel(pos_ref, invf_ref, c_ref, s1_ref, s2_ref):
    ang = pos_ref[...] * invf_ref[...]
    lane = lax.broadcasted_iota(jnp.int32, ang.shape, 1) & (DIFF_DQK - 1)
    c = jnp.cos(ang)
    s = jnp.sin(ang)
    half = ROPE_DIM // 2
    c_ref[...] = jnp.where(lane < ROPE_DIM, c, 1.0)
    s1_ref[...] = jnp.where((lane >= half) & (lane < ROPE_DIM), s, 0.0)
    s2_ref[...] = jnp.where(lane < half, -s, 0.0)


def _rope_tables(positions):
    seq = positions.shape[0]
    half = ROPE_DIM // 2
    inv_freq = ROPE_THETA ** (-jnp.arange(0, ROPE_DIM, 2, dtype=F32) / ROPE_DIM)
    lane = jnp.arange(LANES) % DIFF_DQK
    idx = jnp.where(lane < half, lane, jnp.where(lane < ROPE_DIM, lane - half, 0))
    invf = inv_freq[idx][None, :]
    pos = jnp.broadcast_to(positions.astype(F32)[:, None], (seq, LANES))
    table = jax.ShapeDtypeStruct((seq, LANES), F32)
    return pl.pallas_call(
        _rope_table_kernel,
        out_shape=(table, table, table),
        name="rope_tables",
    )(pos, invf)


def _inproj_kernel(x_ref, nw_ref, wdn_ref, wdf_ref, wg_ref, c_ref, s1_ref, s2_ref, alog_ref, dtb_ref,
                   convw_ref, qkv_ref, z_ref, gate_ref, gate_t_ref, qk_ref, v_ref, pre_ref, *,
                   tiles_per_seq):
    hn = _rms(x_ref[...], nw_ref[...]).astype(BF16)
    tm = x_ref.shape[0]

    @pl.when((pl.program_id(0) % tiles_per_seq) == 0)
    def _():
        pre_ref[0:CONV_CARRY, :] = jnp.zeros((CONV_CARRY, DN_CONV), F32)

    def conv_blocks(first, last):
        for i in range(first, last):
            cols = slice(i * LANES, (i + 1) * LANES)
            cw = convw_ref[:, cols]
            for r0 in range(0, tm, CONV_ROWS):
                win = pre_ref[r0:r0 + CONV_CARRY + CONV_ROWS, cols]
                y = cw[CONV_WIDTH - 1:CONV_WIDTH, :] * win[CONV_CARRY:, :]
                for shift in range(1, CONV_WIDTH):
                    tap = CONV_WIDTH - 1 - shift
                    y = y + cw[tap:tap + 1, :] * pltpu.roll(win, shift, 0)[CONV_CARRY:, :]
                y = _silu(y)
                if i < 2 * DN_HEADS:
                    scale = DN_DK ** -0.5 if i < DN_HEADS else 1.0
                    y = y * (lax.rsqrt(jnp.sum(y * y, axis=-1, keepdims=True) + L2_EPS) * scale)
                qkv_ref[r0:r0 + CONV_ROWS, cols] = y.astype(BF16)

    pre_ref[CONV_CARRY:CONV_CARRY + tm, :] = _dot(hn, wdn_ref[:, 0:DN_CONV])
    z_ref[...] = _dot(hn, wdn_ref[:, DN_CONV:DN_CONV + DN_V]).astype(BF16)
    conv_blocks(0, DN_HEADS)
    qk = _dot(hn, wdf_ref[:, 0:2 * DF_QK])
    conv_blocks(DN_HEADS, 2 * DN_HEADS)
    c, s1, s2 = c_ref[...], s1_ref[...], s2_ref[...]
    half = ROPE_DIM // 2
    for i in range(2 * DF_QK // LANES):
        blk = qk[:, i * LANES:(i + 1) * LANES]
        rot = (blk * c + pltpu.roll(blk, half, 1) * s1 + pltpu.roll(blk, LANES - half, 1) * s2)
        qk_ref[:, i * LANES:(i + 1) * LANES] = rot.astype(BF16)
    v_ref[...] = _dot(hn, wdf_ref[:, 2 * DF_QK:2 * DF_QK + DF_V]).astype(BF16)
    conv_blocks(2 * DN_HEADS, 3 * DN_HEADS)
    pre_ref[0:CONV_CARRY, :] = pre_ref[tm:tm + CONV_CARRY, :]
    gpre = _dot(hn, wg_ref[...])
    lane = lax.broadcasted_iota(jnp.int32, gpre.shape, 1)
    pos_in_chunk = lax.broadcasted_iota(jnp.int32, gpre.shape, 0) & (CHUNK - 1)
    beta = 1.0 / (1.0 + jnp.exp(-gpre))
    t = gpre + dtb_ref[...]
    softplus = jnp.maximum(t, 0.0) + jnp.log1p(jnp.exp(-jnp.abs(t)))
    gcum = -jnp.exp(alog_ref[...]) * softplus
    shift = 1
    while shift < CHUNK:
        gcum = gcum + jnp.where(pos_in_chunk >= shift, pltpu.roll(gcum, shift, 0), 0.0)
        shift *= 2
    gates = jnp.where(lane < DN_HEADS, beta, gcum)
    gate_ref[...] = gates
    gate_t_ref[...] = gates.T[0:N_GATES, :]


def _input_projection(x2d, seq, norm_w, w_dn, w_df, w_gate, tables, alog_row, dtb_row, conv_w):
    tokens = x2d.shape[0]
    tm = min(TOKEN_TILE, seq)
    tiles_per_seq = seq // tm
    row = lambda i: (i, 0)
    pos = lambda i: (i % tiles_per_seq, 0)
    out_shape = (
        jax.ShapeDtypeStruct((tokens, DN_CONV), BF16),
        jax.ShapeDtypeStruct((tokens, DN_V), BF16),
        jax.ShapeDtypeStruct((tokens, LANES), F32),
        jax.ShapeDtypeStruct((N_GATES, tokens), F32),
        jax.ShapeDtypeStruct((tokens, 2 * DF_QK), BF16),
        jax.ShapeDtypeStruct((tokens, DF_V), BF16),
    )
    return pl.pallas_call(
        functools.partial(_inproj_kernel, tiles_per_seq=tiles_per_seq),
        grid=(tokens // tm,),
        in_specs=[
            pl.BlockSpec((tm, D_MODEL), row),
            _resident((1, D_MODEL)),
            _resident((D_MODEL, DN_CONV + DN_V)),
            _resident((D_MODEL, 2 * DF_QK + DF_V)),
            _resident((D_MODEL, LANES)),
            pl.BlockSpec((tm, LANES), pos),
            pl.BlockSpec((tm, LANES), pos),
            pl.BlockSpec((tm, LANES), pos),
            _resident((1, LANES)),
            _resident((1, LANES)),
            _resident((CONV_WIDTH, DN_CONV)),
        ],
        out_specs=(
            pl.BlockSpec((tm, DN_CONV), row),
            pl.BlockSpec((tm, DN_V), row),
            pl.BlockSpec((tm, LANES), row),
            pl.BlockSpec((N_GATES, tm), lambda i: (0, i)),
            pl.BlockSpec((tm, 2 * DF_QK), row),
            pl.BlockSpec((tm, DF_V), row),
        ),
        out_shape=out_shape,
        scratch_shapes=[pltpu.VMEM((CONV_CARRY + tm, DN_CONV), F32)],
        compiler_params=pltpu.CompilerParams(
            dimension_semantics=("arbitrary",), vmem_limit_bytes=VMEM_LIMIT),
        name="input_projection",
    )(x2d, norm_w, w_dn, w_df, w_gate, *tables, alog_row, dtb_row, conv_w)


def _deltanet_kernel(qkv_ref, z_ref, gate_ref, gate_t_ref, nw_ref, gs_ref, out_ref,
                     pq_s, n_s, o_s, dec_s, state_ref, *, seq):
    n_chunks = seq // CHUNK
    C = CHUNK
    ii = lax.broadcasted_iota(jnp.int32, (C, C), 0)
    jj = lax.broadcasted_iota(jnp.int32, (C, C), 1)
    eye = (ii == jj).astype(F32)
    lane = lax.broadcasted_iota(jnp.int32, (C, LANES), 1)

    def precompute(chains):
        st = []
        for c, h in chains:
            r0 = c * C if isinstance(c, int) else pl.multiple_of(c * C, C)
            rows = pl.ds(r0, C)
            qn, kn, v = (qkv_ref[rows, part * DN_QK + h * LANES:part * DN_QK + (h + 1) * LANES]
                         .astype(F32) for part in range(3))
            gates = gate_ref[rows, :]
            beta = jnp.sum(jnp.where(lane == h, gates, 0.0), axis=-1, keepdims=True)
            g_col = jnp.sum(jnp.where(lane == DN_HEADS + h, gates, 0.0), axis=-1, keepdims=True)
            g_row = gate_t_ref[DN_HEADS + h, pl.ds(c, 1), :]
            dec_incl = jnp.exp(jnp.where(ii >= jj, g_col - g_row, MASKED))
            g_last = g_col[C - 1:C, :]
            eg = jnp.exp(g_col)
            kb = kn * beta
            st.append(dict(
                c=c, h=h, rows=rows, dec_incl=dec_incl, qs=qn * eg, kb=kb, knb=kn.astype(BF16),
                lhs=jnp.concatenate([kb, qn], axis=0).astype(BF16),
                rhs=jnp.concatenate([kb * eg, v * beta], axis=1).astype(BF16),
                kd=(kn * jnp.exp(g_last - g_col)).astype(BF16),
                dec=jnp.broadcast_to(jnp.exp(g_last), (SUBLANES, LANES))))
        for d in st:
            kq = _dot_nt(d["lhs"], d["knb"])
            d["qk"] = (kq[C:2 * C, :] * d["dec_incl"]).astype(BF16)
            d["n_pow"] = -(kq[0:C, :] * jnp.where(ii > jj, d["dec_incl"], 0.0))
            d["t_inv"] = eye + d["n_pow"]
        for _ in range(int(math.log2(C)) - 1):
            for d in st:
                nb = d["n_pow"].astype(BF16)
                d["n_pow"] = _dot(nb, nb)
            for d in st:
                d["t_inv"] = d["t_inv"] + _dot(d["t_inv"].astype(BF16), d["n_pow"].astype(BF16))
        for d in st:
            d["wu"] = _dot(d["t_inv"].astype(BF16), d["rhs"]).astype(BF16)
        for d in st:
            d["pn"] = _dot_tn(d["kd"], d["wu"])
            d["qo"] = _dot(d["qk"], d["wu"])
        for d in st:
            c, h, pn, qo = d["c"], d["h"], d["pn"], d["qo"]
            pq_s[h, c, 0:DN_DK, :] = pn[:, 0:DN_DV].astype(BF16)
            pq_s[h, c, DN_DK:DN_DK + C, :] = (d["qs"] - qo[:, 0:DN_DV]).astype(BF16)
            n_s[h, c] = pn[:, DN_DV:2 * DN_DV].astype(BF16)
            o_s[h, d["rows"], :] = qo[:, DN_DV:2 * DN_DV]
            dec_s[h, c] = d["dec"]

    nw = nw_ref[...]

    def recur(group):
        states = [state_ref[h] for h in range(DN_HEADS)]
        for sub in range(DN_UNROLL):
            c = group * DN_UNROLL + sub
            r0 = c * C if isinstance(c, int) else pl.multiple_of(c * C, C)
            rows = pl.ds(r0, C)
            for h in range(DN_HEADS):
                s = states[h]
                ps = _dot(pq_s[h, c], s.astype(BF16))
                states[h] = s * dec_s[h, c][0:1, :] + n_s[h, c].astype(F32) - ps[0:DN_DK, :]
                o = ps[DN_DK:DN_DK + C, :] + o_s[h, rows, :]
                z = z_ref[rows, h * LANES:(h + 1) * LANES].astype(F32)
                y = _rms(o, nw) * _silu(z) * gs_ref[:, h * LANES:(h + 1) * LANES]
                out_ref[rows, h * LANES:(h + 1) * LANES] = y.astype(BF16)
        for h in range(DN_HEADS):
            state_ref[h] = states[h]

    def group_chains(group):
        return [(group * DN_UNROLL + sub, h) for sub in range(DN_UNROLL) for h in range(DN_HEADS)]

    n_groups = n_chunks // DN_UNROLL
    state_ref[...] = jnp.zeros(state_ref.shape, F32)
    precompute(group_chains(0))

    def step(t, carry):
        recur(t - 1)
        precompute(group_chains(t))
        return carry

    lax.fori_loop(1, n_groups, step, 0)
    recur(n_groups - 1)


def _deltanet(qkv, z, gates, gates_t, norm_w, gs):
    batch, seq, _ = qkv.shape
    n_chunks = seq // CHUNK
    assert n_chunks % DN_UNROLL == 0
    per_batch = lambda width: pl.BlockSpec((None, seq, width), lambda b: (b, 0, 0))
    return pl.pallas_call(
        functools.partial(_deltanet_kernel, seq=seq),
        grid=(batch,),
        in_specs=[
            per_batch(DN_CONV),
            per_batch(DN_V),
            per_batch(LANES),
            pl.BlockSpec((N_GATES, n_chunks, CHUNK), lambda b: (0, b, 0)),
            _resident((1, DN_DV)),
            _resident((1, DN_V)),
        ],
        out_specs=per_batch(DN_V),
        out_shape=jax.ShapeDtypeStruct((batch, seq, DN_V), BF16),
        scratch_shapes=[
            pltpu.VMEM((DN_HEADS, n_chunks, DN_DK + CHUNK, DN_DV), BF16),
            pltpu.VMEM((DN_HEADS, n_chunks, DN_DK, DN_DV), BF16),
            pltpu.VMEM((DN_HEADS, seq, DN_DV), F32),
            pltpu.VMEM((DN_HEADS, n_chunks, SUBLANES, LANES), F32),
            pltpu.VMEM((DN_HEADS, DN_DK, DN_DV), F32),
        ],
        compiler_params=pltpu.CompilerParams(
            dimension_semantics=("arbitrary",), vmem_limit_bytes=VMEM_LIMIT),
        name="gated_deltanet",
    )(qkv, z, gates, gates_t, norm_w, gs)


def _diff_attn_kernel(lam_ref, nw_ref, gs_ref, q_ref, k_ref, v_ref, o_ref, vext_ref, *, seq, block,
                      heads, lambda_init):
    lp = lam_ref[...]
    lam = (jnp.exp(jnp.sum(lp[0:1, :] * lp[1:2, :], axis=-1, keepdims=True))
           - jnp.exp(jnp.sum(lp[2:3, :] * lp[3:4, :], axis=-1, keepdims=True)) + lambda_init)
    lane = lax.broadcasted_iota(jnp.int32, (block, LANES), 1)
    first_map = lane < DIFF_DQK
    row = lax.broadcasted_iota(jnp.int32, (block, block), 0)
    col = lax.broadcasted_iota(jnp.int32, (block, block), 1)
    causal = row >= col
    half = block // 2
    scale = DIFF_DQK ** -0.5 * math.log2(math.e)
    zero = jnp.zeros((), BF16)

    head_cols = [slice(hh * LANES, (hh + 1) * LANES) for hh in range(heads)]
    for hh in range(heads):
        vext_ref[hh, :, 0:DIFF_DV] = v_ref[:, head_cols[hh]]
        vext_ref[hh, :, DIFF_DV:2 * DIFF_DV] = jnp.ones((seq, DIFF_DV), BF16)

    def scores(qs, hh, keys, mask):
        kb = k_ref[keys, head_cols[hh]]
        out = [_dot_nt(qm, kb) for qm in qs]
        if mask is not None:
            out = [jnp.where(mask, s, MASKED) for s in out]
        return out

    def absorb(carry, ss, hh, keys):
        vext = vext_ref[hh, keys, :]
        m_new = [jnp.maximum(m, jnp.max(s, axis=-1, keepdims=True)) for s, (m, _) in zip(ss, carry)]
        probs = [jnp.exp2((s - mn).astype(BF16)) for s, mn in zip(ss, m_new)]
        pv = [_dot(p, vext) for p in probs]
        return tuple((mn, jnp.exp2(m - mn) * acc + x) for mn, (m, acc), x in zip(m_new, carry, pv))

    def finish(hh, qi, top, bottom):
        acc1, acc2 = (jnp.concatenate([t[1][0:half, :], b[1]], axis=0) for t, b in zip(top, bottom))
        a1, l1 = acc1[:, 0:DIFF_DV], acc1[:, DIFF_DV:2 * DIFF_DV]
        a2, l2 = acc2[:, 0:DIFF_DV], acc2[:, DIFF_DV:2 * DIFF_DV]
        o = a1 / l1 - lam * (a2 / l2)
        y = _rms(o, nw_ref[...]) * (1.0 - lambda_init) * gs_ref[:, head_cols[hh]]
        o_ref[qi * block:(qi + 1) * block, head_cols[hh]] = y.astype(BF16)

    n_blocks = seq // block
    streams = [(hh, qi) for hh in range(heads) for qi in range(n_blocks)]
    per_stream = []
    for sid, (hh, qi) in enumerate(streams):
        todo = [(sid, "all", slice(j * block, (j + 1) * block), None) for j in range(qi)]
        todo.append((sid, "all", slice(qi * block, qi * block + half), causal[:, 0:half]))
        todo.append((sid, "bottom", slice(qi * block + half, (qi + 1) * block),
                     causal[0:half, 0:half]))
        per_stream.append(todo)
    per_stream.sort(key=len, reverse=True)
    items = []
    while any(per_stream):
        for todo in per_stream:
            if todo:
                items.append(todo.pop(0))

    queries = []
    for hh, qi in streams:
        q = (q_ref[qi * block:(qi + 1) * block, head_cols[hh]].astype(F32) * scale).astype(BF16)
        queries.append((jnp.where(first_map, q, zero), jnp.where(first_map, zero, q)))
    init = (jnp.full((block, 1), MASKED, F32), jnp.zeros((block, 2 * DIFF_DV), F32))
    carries = [(init, init) for _ in streams]

    def item_scores(item):
        sid, part, keys, mask = item
        qs = queries[sid] if part == "all" else tuple(qm[half:, :] for qm in queries[sid])
        return scores(qs, streams[sid][0], keys, mask)

    ss = item_scores(items[0])
    for k, (sid, part, keys, _) in enumerate(items):
        hh, qi = streams[sid]
        ss_next = item_scores(items[k + 1]) if k + 1 < len(items) else None
        if part == "all":
            carries[sid] = absorb(carries[sid], ss, hh, keys)
        else:
            bottom = absorb(tuple((m[half:, :], acc[half:, :]) for m, acc in carries[sid]), ss, hh,
                            keys)
            finish(hh, qi, carries[sid], bottom)
        ss = ss_next


def _diff_attention(qk, v, lam_params, norm_w, gs, lambda_init):
    batch, seq, _ = v.shape
    block = min(ATTN_BLOCK, seq)
    heads = ATTN_HEADS_PER_STEP
    groups = DIFF_HEADS // heads
    head_cols = lambda off: pl.BlockSpec((None, seq, heads * LANES), lambda b, g: (b, 0, off + g))
    return pl.pallas_call(
        functools.partial(_diff_attn_kernel, seq=seq, block=block, heads=heads,
                          lambda_init=lambda_init),
        grid=(batch, groups),
        in_specs=[
            _resident((4, DIFF_DQK)),
            _resident((1, DIFF_DV)),
            pl.BlockSpec((1, heads * LANES), lambda b, g: (0, g)),
            head_cols(0),
            head_cols(groups),
            head_cols(0),
        ],
        out_specs=head_cols(0),
        out_shape=jax.ShapeDtypeStruct((batch, seq, DF_V), BF16),
        scratch_shapes=[pltpu.VMEM((heads, seq, 2 * DIFF_DV), BF16)],
        compiler_params=pltpu.CompilerParams(
            dimension_semantics=("arbitrary", "arbitrary"), vmem_limit_bytes=VMEM_LIMIT),
        name="diff_attention",
    )(lam_params, norm_w, gs, qk, qk, v)


def _out_mlp_kernel(x_ref, odn_ref, odf_ref, wo_ref, nw_ref, wu_ref, wd_ref, fw_ref, y_ref,
                    hidden_ref, *, final_norm):
    x1 = (x_ref[...] + _dot(odn_ref[...], wo_ref[0:DN_V, :])
          + _dot(odf_ref[...], wo_ref[DN_V:DN_V + DF_V, :]))
    hm = _rms(x1, nw_ref[...]).astype(BF16)
    for c in range(D_FF // FF_CHUNK):
        cols = slice(c * FF_CHUNK, (c + 1) * FF_CHUNK)
        hidden_ref[:, cols] = jnp.square(jnp.maximum(_dot(hm, wu_ref[:, cols]), 0.0)).astype(BF16)
    y = x1 + _dot(hidden_ref[...], wd_ref[...])
    if final_norm:
        y = _rms(y, fw_ref[...])
    y_ref[...] = y


def _out_mlp(x2d, o_dn, o_df, w_out, norm_w, w_up, w_down, final_w, final_norm):
    tokens = x2d.shape[0]
    tm = min(MLP_TILE, tokens)
    row = lambda i: (i, 0)
    return pl.pallas_call(
        functools.partial(_out_mlp_kernel, final_norm=final_norm),
        grid=(tokens // tm,),
        in_specs=[
            pl.BlockSpec((tm, D_MODEL), row),
            pl.BlockSpec((tm, DN_V), row),
            pl.BlockSpec((tm, DF_V), row),
            _resident((DN_V + DF_V, D_MODEL)),
            _resident((1, D_MODEL)),
            _resident((D_MODEL, D_FF)),
            _resident((D_FF, D_MODEL)),
            _resident((1, D_MODEL)),
        ],
        out_specs=pl.BlockSpec((tm, D_MODEL), row),
        out_shape=jax.ShapeDtypeStruct((tokens, D_MODEL), F32),
        scratch_shapes=[pltpu.VMEM((tm, D_FF), BF16)],
        compiler_params=pltpu.CompilerParams(
            dimension_semantics=("arbitrary",), vmem_limit_bytes=VMEM_LIMIT),
        name="out_projection_mlp",
    )(x2d, o_dn, o_df, w_out, norm_w, w_up, w_down, final_w)


def _pad_lanes(vec, offset):
    return jnp.zeros((1, LANES), F32).at[0, offset:offset + vec.shape[0]].set(vec.astype(F32))


def kernel(x, positions, attn_norm_w, w_in, conv_w, a_log, dt_bias, dn_norm_w, lambda_q1, lambda_k1, lambda_q2, lambda_k2, diff_norm_w, group_scale, w_out, mlp_norm_w, w_up, w_down, final_norm_w):
    batch, seq, _ = x.shape
    depth = w_in.shape[0]
    tables = _rope_tables(positions)
    x2d = x.reshape(batch * seq, D_MODEL)
    for l in range(depth):
        wl = w_in[l]
        w_dn = wl[:, :GATE_OFF].astype(BF16)
        w_df = wl[:, GATE_OFF + N_GATES:].astype(BF16)
        w_gate = jnp.pad(wl[:, GATE_OFF:GATE_OFF + N_GATES], ((0, 0), (0, LANES - N_GATES))).astype(BF16)
        qkv, z, gates, gates_t, qk, v = _input_projection(
            x2d, seq, attn_norm_w[l][None, :], w_dn, w_df, w_gate, tables,
            _pad_lanes(a_log[l], DN_HEADS), _pad_lanes(dt_bias[l], DN_HEADS), conv_w[l].astype(F32))
        gs = group_scale[l].astype(F32)[None, :]
        o_dn = _deltanet(qkv.reshape(batch, seq, DN_CONV), z.reshape(batch, seq, DN_V),
                         gates.reshape(batch, seq, LANES),
                         gates_t.reshape(N_GATES, batch * seq // CHUNK, CHUNK),
                         dn_norm_w[l].astype(F32)[None, :], gs[:, :DN_V])
        lam_params = jnp.stack([lambda_q1[l], lambda_k1[l], lambda_q2[l], lambda_k2[l]]).astype(F32)
        lambda_init = 0.8 - 0.6 * math.exp(-0.3 * l)
        o_df = _diff_attention(qk.reshape(batch, seq, 2 * DF_QK), v.reshape(batch, seq, DF_V),
                               lam_params, diff_norm_w[l].astype(F32)[None, :], gs[:, DN_V:],
                               lambda_init)
        x2d = _out_mlp(x2d, o_dn.reshape(batch * seq, DN_V), o_df.reshape(batch * seq, DF_V),
                       w_out[l].astype(BF16), mlp_norm_w[l][None, :], w_up[l].astype(BF16),
                       w_down[l].astype(BF16), final_norm_w[None, :], l == depth - 1)
    return x2d.reshape(batch, seq, D_MODEL)
```
